```python
import math
import jax, jax.numpy as jnp
from jax import lax
import numpy as np

D_MODEL = 2048
BATCH = 4
SEQ = 8192
DEPTH = 4
DEC_BATCH = 1
DEC_SEQ = 16384
PAST_LEN = 128

N_MIXERS = 3
GRID_W = 64
EPS = 1e-6
POOL_WINDOWS = (2, 4, 8, 16)
N_POOL_GROUPS = len(POOL_WINDOWS)
POOL_GROUP = D_MODEL // N_POOL_GROUPS
HYENA_ORDER = 2
HYENA_EMB = 33
HYENA_BANDS = (HYENA_EMB - 1) // 2
HYENA_HIDDEN = 64
HYENA_FAST_DECAY = 0.3
HYENA_SLOW_DECAY = 1.5
HYENA_TARGET = 1e-2
HYENA_MOD_SHIFT = 0.05
HYENA_MIN_DECAY = math.log(HYENA_TARGET) / HYENA_SLOW_DECAY
HYENA_MAX_DECAY = math.log(HYENA_TARGET) / HYENA_FAST_DECAY
HEAD_DIM = 128
N_HEADS = D_MODEL // HEAD_DIM
N_KV_HEADS = N_HEADS // 4
GQA_GROUP = N_HEADS // N_KV_HEADS
ROPE_AXIS_DIM = HEAD_DIM // 2
ROPE_THETA = 10000.0
Q_BLOCK = 128
D_FF = 5632
N_LAYERS_POOL = (DEPTH + 2) // 3
N_LAYERS_HYENA = (DEPTH + 1) // 3
N_LAYERS_ATTN = DEPTH // 3

kernel_name = 'hybrid_pool_hyena_axialgqa_encoder'


def rms_norm(x, g):
    xf = x.astype(jnp.float32)
    y = xf * lax.rsqrt(jnp.mean(xf * xf, axis=-1, keepdims=True) + EPS)
    return (y * g.astype(jnp.float32)).astype(x.dtype)


def dwconv3(u, w, b):
    up = jnp.pad(u, ((0, 0), (1, 1), (0, 0)))
    return up[:, :-2] * w[0] + up[:, 1:-1] * w[1] + up[:, 2:] * w[2] + b


def pool_mixer(h, w, scale):
    B, L, D = h.shape
    hg = h.reshape(B, L, N_POOL_GROUPS, POOL_GROUP).astype(jnp.float32)
    t = jnp.arange(L)
    zeros = jnp.zeros((B, 1, POOL_GROUP), jnp.float32)
    outs = []
    for g, win in enumerate(POOL_WINDOWS):
        half = win // 2
        lo = jnp.clip(t - half, 0, L - 1)
        hi = jnp.clip(t + win - half - 1, 0, L - 1)
        xg = hg[:, :, g]
        cs = jnp.concatenate([zeros, jnp.cumsum(xg, axis=1)], axis=1)
        s = jnp.take(cs, hi + 1, axis=1) - jnp.take(cs, lo, axis=1)
        cnt = (hi - lo + 1).astype(jnp.float32)[None, :, None]
        outs.append(s / cnt - xg)
    p = jnp.stack(outs, axis=2)
    y = jnp.einsum('blgc,gcd->blgd', p, w.astype(jnp.float32))
    return (y.reshape(B, L, D) * scale).astype(h.dtype)


def hyena_pos_features(L):
    t = jnp.linspace(0.0, 1.0, L, dtype=jnp.float32)[:, None]
    w = 2.0 * math.pi * jnp.arange(L, dtype=jnp.float32)[:, None] / L
    f = jnp.linspace(1e-4, HYENA_BANDS - 1, HYENA_BANDS, dtype=jnp.float32)[None, :]
    z = jnp.concatenate([t, jnp.cos(f * w), -jnp.sin(f * w)], axis=-1)
    deltas = jnp.abs(jnp.linspace(HYENA_MIN_DECAY, HYENA_MAX_DECAY, D_MODEL, dtype=jnp.float32))
    decay = jnp.exp(-t * deltas[None, :]) + HYENA_MOD_SHIFT
    return z, decay


def hyena_mixer(h, z, decay, w_in, b_in, conv_w, conv_b, f_w1, f_b1, f_w2, f_b2, f_w3, f_freq, skip, w_out):
    B, L, D = h.shape
    u = dwconv3(h @ w_in + b_in, conv_w, conv_b)
    v, x1, x2 = jnp.split(u.astype(jnp.float32), 3, axis=-1)
    a = jnp.sin(f_freq * (z @ f_w1 + f_b1))
    a = jnp.sin(f_freq * (a @ f_w2 + f_b2))
    filt = (a @ f_w3).reshape(L, HYENA_ORDER, 2, D) * decay[:, None, None, :]
    n = 2 * L
    zz = v
    for o, gate in enumerate((x1, x2)):
        k = jnp.concatenate([filt[:, o, 0], jnp.zeros((1, D), jnp.float32), filt[:0:-1, o, 1]], axis=0)
        kf = jnp.fft.rfft(k.astype(jnp.float32), n=n, axis=0)
        zf = jnp.fft.rfft(zz, n=n, axis=1)
        conv = jnp.fft.irfft(zf * kf[None], n=n, axis=1)[:, :L]
        zz = gate * (conv + zz * skip[o])
    return zz.astype(h.dtype) @ w_out


def axial_rope_tables(L):
    rows = L // GRID_W
    row = jnp.repeat(jnp.arange(rows, dtype=jnp.float32), GRID_W)
    col = jnp.tile(jnp.arange(GRID_W, dtype=jnp.float32), rows)
    inv = ROPE_THETA ** (-jnp.arange(0, ROPE_AXIS_DIM, 2, dtype=jnp.float32) / ROPE_AXIS_DIM)
    ang = jnp.stack([row[:, None] * inv, col[:, None] * inv], axis=1)
    return jnp.cos(ang), jnp.sin(ang)


def apply_rope(x, cos, sin):
    B, L, H, _ = x.shape
    xr = x.astype(jnp.float32).reshape(B, L, H, 2, 2, ROPE_AXIS_DIM // 2)
    x1 = xr[..., 0, :]
    x2 = xr[..., 1, :]
    c = cos[None, :, None]
    s = sin[None, :, None]
    out = jnp.stack([x1 * c - x2 * s, x2 * c + x1 * s], axis=-2)
    return out.reshape(B, L, H, HEAD_DIM).astype(x.dtype)


def attention_mixer(h, cos, sin, w_qkv, q_gain, k_gain, w_o):
    B, L, D = h.shape
    qkv = h @ w_qkv
    q, k, v = jnp.split(qkv, [N_HEADS * HEAD_DIM, (N_HEADS + N_KV_HEADS) * HEAD_DIM], axis=-1)
    q = apply_rope(rms_norm(q.reshape(B, L, N_HEADS, HEAD_DIM), q_gain), cos, sin)
    k = apply_rope(rms_norm(k.reshape(B, L, N_KV_HEADS, HEAD_DIM), k_gain), cos, sin)
    v = v.reshape(B, L, N_KV_HEADS, HEAD_DIM)
    nb = L // Q_BLOCK
    qb = q.reshape(B, nb, Q_BLOCK, N_KV_HEADS, GQA_GROUP, HEAD_DIM).transpose(1, 0, 3, 4, 2, 5)
    kt = k.transpose(0, 2, 1, 3)
    vt = v.transpose(0, 2, 1, 3)
    scale = HEAD_DIM ** -0.5

    def block(qblk):
        s = jnp.einsum('bkgqd,bksd->bkgqs', qblk, kt).astype(jnp.float32) * scale
        p = jax.nn.softmax(s, axis=-1)
        return jnp.einsum('bkgqs,bksd->bkgqd', p.astype(vt.dtype), vt)

    o = lax.map(block, qb)
    o = o.transpose(1, 0, 4, 2, 3, 5).reshape(B, L, D)
    return o @ w_o


def conv_glu(h, w_up, conv_w, conv_b, w_down):
    a, b = jnp.split(h @ w_up, 2, axis=-1)
    a = dwconv3(a, conv_w, conv_b)
    return (jax.nn.gelu(a, approximate=True) * b) @ w_down


def encoder_trunk(x, c, mod_w, mod_b, norm_mix_pre, norm_mix_post, norm_ffn_pre, norm_ffn_post,
                  ffn_w_up, ffn_conv_w, ffn_conv_b, ffn_w_down, pool_w, pool_scale,
                  hy_w_in, hy_b_in, hy_conv_w, hy_conv_b, hy_f_w1, hy_f_b1, hy_f_w2, hy_f_b2,
                  hy_f_w3, hy_f_freq, hy_skip, hy_w_out, at_w_qkv, at_q_gain, at_k_gain, at_w_o):
    L = x.shape[1]
    z, decay = hyena_pos_features(L)
    cos, sin = axial_rope_tables(L)
    cact = jax.nn.silu(c.astype(jnp.float32))
    for i in range(DEPTH):
        mod = (cact @ mod_w[i] + mod_b[i])[:, None, :]
        sh1, sc1, g1, sh2, sc2, g2 = jnp.split(mod, 6, axis=-1)
        h = (rms_norm(x, norm_mix_pre[i]) * (1.0 + sc1) + sh1).astype(x.dtype)
        kind, j = i % N_MIXERS, i // N_MIXERS
        if kind == 0:
            m = pool_mixer(h, pool_w[j], pool_scale[j])
        elif kind == 1:
            m = hyena_mixer(h, z, decay, hy_w_in[j], hy_b_in[j], hy_conv_w[j], hy_conv_b[j],
                            hy_f_w1[j], hy_f_b1[j], hy_f_w2[j], hy_f_b2[j], hy_f_w3[j], hy_f_freq[j],
                            hy_skip[j], hy_w_out[j])
        else:
            m = attention_mixer(h, cos, sin, at_w_qkv[j], at_q_gain[j], at_k_gain[j], at_w_o[j])
        x = x + (g1 * rms_norm(m, norm_mix_post[i])).astype(x.dtype)
        h = (rms_norm(x, norm_ffn_pre[i]) * (1.0 + sc2) + sh2).astype(x.dtype)
        f = conv_glu(h, ffn_w_up[i], ffn_conv_w[i], ffn_conv_b[i], ffn_w_down[i])
        x = x + (g2 * rms_norm(f, norm_ffn_post[i])).astype(x.dtype)
    return x


def setup_inputs(seed: int = 0) -> dict:
    key = jax.random.key(seed)
    ks = iter(jax.random.split(key, 40))
    D = D_MODEL

    def nrm(shape, scale):
        return jax.random.normal(next(ks), shape, jnp.float32) * scale

    def gain(shape, s=0.05):
        return 1.0 + nrm(shape, s)

    qkv_out = (N_HEADS + 2 * N_KV_HEADS) * HEAD_DIM
    return {
        'x_prompt': nrm((BATCH, SEQ, D), 1.0),
        'x_sample': nrm((DEC_BATCH, DEC_SEQ, D), 1.0),
        'c_prompt': nrm((BATCH, D), 1.0),
        'c_sample': nrm((DEC_BATCH, D), 1.0),
        'mod_w': nrm((DEPTH, D, 6 * D), 0.5 * D ** -0.5),
        'mod_b': nrm((DEPTH, 6 * D), 0.02),
        'norm_mix_pre': gain((DEPTH, D)),
        'norm_mix_post': gain((DEPTH, D)),
        'norm_ffn_pre': gain((DEPTH, D)),
        'norm_ffn_post': gain((DEPTH, D)),
        'ffn_w_up': nrm((DEPTH, D, 2 * D_FF), D ** -0.5),
        'ffn_conv_w': nrm((DEPTH, 3, D_FF), 3 ** -0.5),
        'ffn_conv_b': nrm((DEPTH, D_FF), 0.02),
        'ffn_w_down': nrm((DEPTH, D_FF, D), D_FF ** -0.5),
        'pool_w': nrm((N_LAYERS_POOL, N_POOL_GROUPS, POOL_GROUP, POOL_GROUP), POOL_GROUP ** -0.5),
        'pool_scale': gain((N_LAYERS_POOL, D), 0.1),
        'hy_w_in': nrm((N_LAYERS_HYENA, D, 3 * D), D ** -0.5),
        'hy_b_in': nrm((N_LAYERS_HYENA, 3 * D), 0.02),
        'hy_conv_w': nrm((N_LAYERS_HYENA, 3, 3 * D), 3 ** -0.5),
        'hy_conv_b': nrm((N_LAYERS_HYENA, 3 * D), 0.02),
        'hy_f_w1': nrm((N_LAYERS_HYENA, HYENA_EMB, HYENA_HIDDEN), HYENA_EMB ** -0.5),
        'hy_f_b1': nrm((N_LAYERS_HYENA, HYENA_HIDDEN), 0.1),
        'hy_f_w2': nrm((N_LAYERS_HYENA, HYENA_HIDDEN, HYENA_HIDDEN), HYENA_HIDDEN ** -0.5),
        'hy_f_b2': nrm((N_LAYERS_HYENA, HYENA_HIDDEN), 0.1),
        'hy_f_w3': nrm((N_LAYERS_HYENA, HYENA_HIDDEN, HYENA_ORDER * 2 * D), HYENA_HIDDEN ** -0.5),
        'hy_f_freq': gain((N_LAYERS_HYENA, HYENA_HIDDEN)),
        'hy_skip': nrm((N_LAYERS_HYENA, HYENA_ORDER, D), 0.5),
        'hy_w_out': nrm((N_LAYERS_HYENA, D, D), D ** -0.5),
        'at_w_qkv': nrm((N_LAYERS_ATTN, D, qkv_out), D ** -0.5),
        'at_q_gain': gain((N_LAYERS_ATTN, HEAD_DIM)),
        'at_k_gain': gain((N_LAYERS_ATTN, HEAD_DIM)),
        'at_w_o': nrm((N_LAYERS_ATTN, D, D), D ** -0.5),
    }


def reference(x_prompt, x_sample, c_prompt, c_sample, mod_w, mod_b, norm_mix_pre, norm_mix_post,
              norm_ffn_pre, norm_ffn_post, ffn_w_up, ffn_conv_w, ffn_conv_b, ffn_w_down,
              pool_w, pool_scale, hy_w_in, hy_b_in, hy_conv_w, hy_conv_b, hy_f_w1, hy_f_b1,
              hy_f_w2, hy_f_b2, hy_f_w3, hy_f_freq, hy_skip, hy_w_out,
              at_w_qkv, at_q_gain, at_k_gain, at_w_o):
    y_prompt = encoder_trunk(x_prompt, c_prompt, mod_w, mod_b, norm_mix_pre, norm_mix_post,
                             norm_ffn_pre, norm_ffn_post, ffn_w_up, ffn_conv_w, ffn_conv_b, ffn_w_down,
                             pool_w, pool_scale, hy_w_in, hy_b_in, hy_conv_w, hy_conv_b, hy_f_w1, hy_f_b1,
                             hy_f_w2, hy_f_b2, hy_f_w3, hy_f_freq, hy_skip, hy_w_out,
                             at_w_qkv, at_q_gain, at_k_gain, at_w_o)
    y_sample = encoder_trunk(x_sample, c_sample, mod_w, mod_b, norm_mix_pre, norm_mix_post,
                             norm_ffn_pre, norm_ffn_post, ffn_w_up, ffn_conv_w, ffn_conv_b, ffn_w_down,
                             pool_w, pool_scale, hy_w_in, hy_b_in, hy_conv_w, hy_conv_b, hy_f_w1, hy_f_b1,
                             hy_f_w2, hy_f_b2, hy_f_w3, hy_f_freq, hy_skip, hy_w_out,
                             at_w_qkv, at_q_gain, at_k_gain, at_w_o)
    return (y_prompt, y_sample)
```

```python
import functools
import math

import jax
import jax.numpy as jnp
from jax import lax
from jax.experimental import pallas as pl
from jax.experimental.pallas import tpu as pltpu

EPS = 1e-6
HEAD_DIM = 128
GQA_GROUP = 4
GRID_W = 64
ROPE_THETA = 10000.0
POOL_WINDOWS = (2, 4, 8, 16)
HYENA_ORDER = 2
HYENA_EMB = 33
HYENA_BANDS = (HYENA_EMB - 1) // 2
HYENA_EMB_PAD = 64
HYENA_FAST_DECAY = 0.3
HYENA_SLOW_DECAY = 1.5
HYENA_TARGET = 1e-2
HYENA_MOD_SHIFT = 0.05
HYENA_MIN_DECAY = math.log(HYENA_TARGET) / HYENA_SLOW_DECAY
HYENA_MAX_DECAY = math.log(HYENA_TARGET) / HYENA_FAST_DECAY

HALO = 16
MOD_ROWS = 8
DFT_N2_MAX = 512
DFT_N1_MIN = 16
VMEM_LIMIT = 52 * 1024 * 1024

F32 = jnp.float32
BF16 = jnp.bfloat16


def _cparams(*sem):
    return pltpu.CompilerParams(dimension_semantics=sem, vmem_limit_bytes=VMEM_LIMIT)


def _tile(dim, pref):
    t = min(dim, pref)
    while dim % t:
        t //= 2
    return t


def _split3(x):
    hi = x.astype(BF16)
    lo = (x - hi.astype(F32)).astype(BF16)
    return hi, lo


def _dot3(a, b):
    ah, al = _split3(a)
    bh, bl = _split3(b)
    d = functools.partial(jnp.dot, preferred_element_type=F32)
    return d(ah, bh) + (d(ah, bl) + d(al, bh))


def _dot3_const(mcat, x, axis=0):
    xh, xl = _split3(x)
    return jnp.dot(mcat, jnp.concatenate([xh, xl, xh], axis=axis), preferred_element_type=F32)


def _cat3(m):
    mh, ml = _split3(m)
    return jnp.concatenate([mh, mh, ml], axis=-1)


def _modnorm(xv, gain, scale1p, shift):
    ms = jnp.mean(xv * xv, axis=-1, keepdims=True)
    return xv * lax.rsqrt(ms + EPS) * gain * scale1p + shift


def _halo_maps(tm, t_rows):
    r = tm // HALO
    last = t_rows // HALO - 1
    prev = lambda i, *_: (jnp.maximum(i * r - 1, 0), 0)
    nxt = lambda i, *_: (jnp.minimum((i + 1) * r, last), 0)
    return prev, nxt


def _mod_kernel(c_ref, w_ref, b_ref, o_ref):
    c = c_ref[...]
    a = (c / (1.0 + jnp.exp(-c))).astype(BF16)
    o_ref[0] = jnp.dot(a, w_ref[0].astype(BF16), preferred_element_type=F32) + b_ref[0]


def _modulation(c8, mod_w, mod_b):
    depth, d, n = mod_w.shape
    tn = _tile(n, 1024)
    return pl.pallas_call(
        _mod_kernel,
        out_shape=jax.ShapeDtypeStruct((depth, MOD_ROWS, n), F32),
        grid=(depth, n // tn),
        in_specs=[
            pl.BlockSpec((MOD_ROWS, d), lambda l, j: (0, 0)),
            pl.BlockSpec((1, d, tn), lambda l, j: (l, 0, j)),
            pl.BlockSpec((1, 1, tn), lambda l, j: (l, 0, j)),
        ],
        out_specs=pl.BlockSpec((1, MOD_ROWS, tn), lambda l, j: (l, 0, j)),
        compiler_params=_cparams("arbitrary", "arbitrary"),
        name="modulation",
    )(c8, mod_w, mod_b.reshape(depth, 1, n))


def _fill_h(h_scr, x_ref, xp_ref, xn_ref, mod_ref, g_ref, sh_idx, sc_idx, tm):
    m = mod_ref[0]
    shift = m[sh_idx:sh_idx + 1]
    scale1p = 1.0 + m[sc_idx:sc_idx + 1]
    gain = g_ref[...]
    h_scr[0:HALO] = _modnorm(xp_ref[...], gain, scale1p, shift).astype(h_scr.dtype)
    h_scr[HALO:HALO + tm] = _modnorm(x_ref[...], gain, scale1p, shift).astype(h_scr.dtype)
    h_scr[HALO + tm:] = _modnorm(xn_ref[...], gain, scale1p, shift).astype(h_scr.dtype)


def _edge_keep(tps):
    i = pl.program_id(0) % tps
    return (i != 0).astype(F32), (i != tps - 1).astype(F32)


def _conv3_rows(a, keep_first, keep_last, cw, cb, tm):
    rows = tm + 2 * HALO
    a = jnp.concatenate([a[0:HALO] * keep_first, a[HALO:HALO + tm], a[HALO + tm:] * keep_last], axis=0)
    prev = pltpu.roll(a, 1, 0)
    nxt = pltpu.roll(a, rows - 1, 0)
    c = prev * cw[0:1] + a * cw[1:2] + nxt * cw[2:3] + cb
    return c[HALO:HALO + tm]


def _gelu_tanh(x):
    return 0.5 * x * (1.0 + jnp.tanh(0.7978845608028654 * (x + 0.044715 * (x * x * x))))


def _up_kernel(*refs, sh_idx, sc_idx, tm, tps, glu, has_bias):
    x_ref, xp_ref, xn_ref, mod_ref, g_ref, w_ref = refs[:6]
    k = 6
    wb_ref = b_ref = None
    if glu:
        wb_ref = refs[k]
        k += 1
    if has_bias:
        b_ref = refs[k]
        k += 1
    cw_ref, cb_ref, o_ref, h_scr = refs[k:k + 4]

    @pl.when(pl.program_id(1) == 0)
    def _():
        _fill_h(h_scr, x_ref, xp_ref, xn_ref, mod_ref, g_ref, sh_idx, sc_idx, tm)

    a = jnp.dot(h_scr[...], w_ref[...], preferred_element_type=F32)
    if has_bias:
        a = a + b_ref[...]
    keep_first, keep_last = _edge_keep(tps)
    c = _conv3_rows(a, keep_first, keep_last, cw_ref[...], cb_ref[...], tm)
    if glu:
        b = jnp.dot(h_scr[HALO:HALO + tm], wb_ref[...], preferred_element_type=F32)
        c = _gelu_tanh(c) * b
    o_ref[...] = c.astype(o_ref.dtype)


def _up(x, mod, gain, w, bias, cw, cb, *, seq, sh_idx, sc_idx, glu, out_dtype):
    t_rows, d = x.shape
    n_out = cw.shape[-1]
    tm = _tile(seq, 512)
    tn = _tile(n_out, 512)
    tps = seq // tm
    nj = n_out // tn
    prev, nxt = _halo_maps(tm, t_rows)
    in_specs = [
        pl.BlockSpec((tm, d), lambda i, j: (i, 0)),
        pl.BlockSpec((HALO, d), prev),
        pl.BlockSpec((HALO, d), nxt),
        pl.BlockSpec((1, 6, d), lambda i, j: (i // tps, 0, 0)),
        pl.BlockSpec((1, d), lambda i, j: (0, 0)),
        pl.BlockSpec((d, tn), lambda i, j: (0, j)),
    ]
    args = [x, x, x, mod, gain, w]
    if glu:
        in_specs.append(pl.BlockSpec((d, tn), lambda i, j: (0, j + nj)))
        args.append(w)
    if bias is not None:
        in_specs.append(pl.BlockSpec((1, tn), lambda i, j: (0, j)))
        args.append(bias)
    in_specs += [pl.BlockSpec((3, tn), lambda i, j: (0, j)), pl.BlockSpec((1, tn), lambda i, j: (0, j))]
    args += [cw, cb]
    return pl.pallas_call(
        functools.partial(_up_kernel, sh_idx=sh_idx, sc_idx=sc_idx, tm=tm, tps=tps, glu=glu,
                          has_bias=bias is not None),
        out_shape=jax.ShapeDtypeStruct((t_rows, n_out), out_dtype),
        grid=(t_rows // tm, nj),
        in_specs=in_specs,
        out_specs=pl.BlockSpec((tm, tn), lambda i, j: (i, j)),
        scratch_shapes=[pltpu.VMEM((tm + 2 * HALO, d), BF16)],
        compiler_params=_cparams("arbitrary", "arbitrary"),
        name="up_glu" if glu else "up_conv",
    )(*args)


def _down_kernel(g_ref, w_ref, x_ref, mod_ref, gain_ref, o_ref, acc_ref, *, gate_idx, nk):
    k = pl.program_id(1)
    part = jnp.dot(g_ref[...], w_ref[...], preferred_element_type=F32)

    @pl.when(k == 0)
    def _():
        acc_ref[...] = part

    @pl.when(k > 0)
    def _():
        acc_ref[...] += part

    @pl.when(k == nk - 1)
    def _():
        f = acc_ref[...]
        ms = jnp.mean(f * f, axis=-1, keepdims=True)
        y = f * lax.rsqrt(ms + EPS) * gain_ref[...]
        gate = mod_ref[0][gate_idx:gate_idx + 1]
        o_ref[...] = x_ref[...] + gate * y


def _down(g, w, x, mod, gain, *, seq, gate_idx):
    t_rows, kdim = g.shape
    d = w.shape[1]
    tm = _tile(seq, 512)
    tk = _tile(kdim, 512)
    tps = seq // tm
    nk = kdim // tk
    return pl.pallas_call(
        functools.partial(_down_kernel, gate_idx=gate_idx, nk=nk),
        out_shape=jax.ShapeDtypeStruct((t_rows, d), F32),
        grid=(t_rows // tm, nk),
        in_specs=[
            pl.BlockSpec((tm, tk), lambda i, k: (i, k)),
            pl.BlockSpec((tk, d), lambda i, k: (k, 0)),
            pl.BlockSpec((tm, d), lambda i, k: (i, 0)),
            pl.BlockSpec((1, 6, d), lambda i, k: (i // tps, 0, 0)),
            pl.BlockSpec((1, d), lambda i, k: (0, 0)),
        ],
        out_specs=pl.BlockSpec((tm, d), lambda i, k: (i, 0)),
        scratch_shapes=[pltpu.VMEM((tm, d), F32)],
        compiler_params=_cparams("arbitrary", "arbitrary"),
        name="down",
    )(g, w, x, mod, gain)


def _pool_kernel(x_ref, xp_ref, xn_ref, mod_ref, gpre_ref, gpost_ref, pw_ref, ps_ref, o_ref, h_scr,
                 *, tm, tps, seq):
    _fill_h(h_scr, x_ref, xp_ref, xn_ref, mod_ref, gpre_ref, 0, 1, tm)
    keep_first, keep_last = _edge_keep(tps)
    rows = tm + 2 * HALO
    d = x_ref.shape[1]
    cg = d // len(POOL_WINDOWS)
    pos = (pl.program_id(0) % tps) * tm + lax.broadcasted_iota(jnp.int32, (tm, 1), 0)
    ys = []
    ssq = jnp.zeros((tm, 1), F32)
    for g, win in enumerate(POOL_WINDOWS):
        half = win // 2
        sl = slice(g * cg, (g + 1) * cg)
        hg = jnp.concatenate([h_scr[0:HALO, sl] * keep_first, h_scr[HALO:HALO + tm, sl],
                              h_scr[HALO + tm:, sl] * keep_last], axis=0)
        p = hg
        s = 1
        while s < win:
            p = p + pltpu.roll(p, s, 0)
            s *= 2
        if half > 1:
            p = pltpu.roll(p, rows - (half - 1), 0)
        lo = jnp.maximum(pos - half, 0)
        hi = jnp.minimum(pos + (half - 1), seq - 1)
        cnt = (hi - lo + 1).astype(F32)
        pooled = p[HALO:HALO + tm] / cnt - hg[HALO:HALO + tm]
        y = jnp.dot(pooled.astype(BF16), pw_ref[g], preferred_element_type=F32) * ps_ref[:, sl]
        ssq = ssq + jnp.sum(y * y, axis=-1, keepdims=True)
        ys.append(y)
    inv = lax.rsqrt(ssq / d + EPS)
    gate = mod_ref[0][2:3]
    for g in range(len(POOL_WINDOWS)):
        sl = slice(g * cg, (g + 1) * cg)
        o_ref[:, sl] = x_ref[:, sl] + gate[:, sl] * (ys[g] * inv * gpost_ref[:, sl])


def _pool_layer(x, mod, gpre, gpost, pw, ps, *, seq):
    t_rows, d = x.shape
    tm = _tile(seq, 256)
    tps = seq // tm
    prev, nxt = _halo_maps(tm, t_rows)
    return pl.pallas_call(
        functools.partial(_pool_kernel, tm=tm, tps=tps, seq=seq),
        out_shape=jax.ShapeDtypeStruct((t_rows, d), F32),
        grid=(t_rows // tm,),
        in_specs=[
            pl.BlockSpec((tm, d), lambda i: (i, 0)),
            pl.BlockSpec((HALO, d), prev),
            pl.BlockSpec((HALO, d), nxt),
            pl.BlockSpec((1, 6, d), lambda i: (i // tps, 0, 0)),
            pl.BlockSpec((1, d), lambda i: (0, 0)),
            pl.BlockSpec((1, d), lambda i: (0, 0)),
            pl.BlockSpec(pw.shape, lambda i: (0, 0, 0)),
            pl.BlockSpec((1, d), lambda i: (0, 0)),
        ],
        out_specs=pl.BlockSpec((tm, d), lambda i: (i, 0)),
        scratch_shapes=[pltpu.VMEM((tm + 2 * HALO, d), F32)],
        compiler_params=_cparams("arbitrary"),
        name="pool_layer",
    )(x, x, x, mod, gpre, gpost, pw, ps)


def _rope_tables(seq):
    pos = jnp.arange(seq, dtype=jnp.int32)
    row = (pos // GRID_W).astype(F32)[:, None]
    col = (pos % GRID_W).astype(F32)[:, None]
    axis_dim = HEAD_DIM // 2
    inv = ROPE_THETA ** (-jnp.arange(0, axis_dim, 2, dtype=F32) / axis_dim)
    ar, ac = row * inv, col * inv
    cos = jnp.concatenate([jnp.cos(ar), jnp.cos(ar), jnp.cos(ac), jnp.cos(ac)], axis=-1)
    sin = jnp.concatenate([-jnp.sin(ar), jnp.sin(ar), -jnp.sin(ac), jnp.sin(ac)], axis=-1)
    return cos, sin


def _qkv_kernel(x_ref, mod_ref, g_ref, w_ref, hg_ref, cos_ref, sin_ref, o_ref, h_scr, *, n_qk_tiles, n_q_tiles):
    j = pl.program_id(1)

    @pl.when(j == 0)
    def _():
        m = mod_ref[0]
        h_scr[...] = _modnorm(x_ref[...], g_ref[...], 1.0 + m[1:2], m[0:1]).astype(BF16)

    a = jnp.dot(h_scr[...], w_ref[...], preferred_element_type=F32)

    @pl.when(j < n_qk_tiles)
    def _():
        is_q = j < n_q_tiles
        gain = jnp.where(is_q, hg_ref[0:1], hg_ref[1:2])
        out_scale = jnp.where(is_q, HEAD_DIM ** -0.5, 1.0)
        cos = cos_ref[...]
        sin = sin_ref[...]
        quarter = HEAD_DIM // 4
        lane = lax.broadcasted_iota(jnp.int32, (1, HEAD_DIM), 1)
        low = (lane % (2 * quarter)) < quarter
        for h in range(a.shape[1] // HEAD_DIM):
            sl = slice(h * HEAD_DIM, (h + 1) * HEAD_DIM)
            v = a[:, sl]
            ms = jnp.mean(v * v, axis=-1, keepdims=True)
            v = v * lax.rsqrt(ms + EPS) * gain
            swapped = jnp.where(low, pltpu.roll(v, HEAD_DIM - quarter, 1), pltpu.roll(v, quarter, 1))
            o_ref[:, sl] = ((v * cos + swapped * sin) * out_scale).astype(o_ref.dtype)

    @pl.when(j >= n_qk_tiles)
    def _():
        o_ref[...] = a.astype(o_ref.dtype)


def _qkv(x, mod, gain, w, head_gains, cos, sin, *, seq, n_kv):
    t_rows, d = x.shape
    n_out = w.shape[1]
    tm = _tile(seq, 512)
    tn = HEAD_DIM * n_kv
    tps = seq // tm
    n_q_tiles = d // tn
    return pl.pallas_call(
        functools.partial(_qkv_kernel, n_qk_tiles=n_q_tiles + 1, n_q_tiles=n_q_tiles),
        out_shape=jax.ShapeDtypeStruct((t_rows, n_out), BF16),
        grid=(t_rows // tm, n_out // tn),
        in_specs=[
            pl.BlockSpec((tm, d), lambda i, j: (i, 0)),
            pl.BlockSpec((1, 6, d), lambda i, j: (i // tps, 0, 0)),
            pl.BlockSpec((1, d), lambda i, j: (0, 0)),
            pl.BlockSpec((d, tn), lambda i, j: (0, j)),
            pl.BlockSpec((2, HEAD_DIM), lambda i, j: (0, 0)),
            pl.BlockSpec((tm, HEAD_DIM), lambda i, j: (i % tps, 0)),
            pl.BlockSpec((tm, HEAD_DIM), lambda i, j: (i % tps, 0)),
        ],
        out_specs=pl.BlockSpec((tm, tn), lambda i, j: (i, j)),
        scratch_shapes=[pltpu.VMEM((tm, d), BF16)],
        compiler_params=_cparams("arbitrary", "arbitrary"),
        name="qkv_rope",
    )(x, mod, gain, w, head_gains, cos, sin)


def _flash_kernel(q_ref, k_ref, v_ref, o_ref, *, tq, tk, seq):
    q = q_ref[...]
    qs = jnp.concatenate([q[:, h * HEAD_DIM:(h + 1) * HEAD_DIM] for h in range(GQA_GROUP)], axis=0)
    rows = GQA_GROUP * tq

    def step(kk, carry):
        m, l, acc = carry
        start = pl.multiple_of(kk * tk, tk)
        ks = k_ref[pl.ds(start, tk), :]
        vs = v_ref[pl.ds(start, tk), :]
        s = lax.dot_general(qs, ks, (((1,), (1,)), ((), ())), preferred_element_type=F32)
        m_new = jnp.maximum(m, jnp.max(s, axis=-1, keepdims=True))
        alpha = jnp.exp(m - m_new)
        p = jnp.exp(s - m_new)
        l = alpha * l + jnp.sum(p, axis=-1, keepdims=True)
        acc = alpha * acc + jnp.dot(p.astype(BF16), vs, preferred_element_type=F32)
        return m_new, l, acc

    m0 = jnp.full((rows, 1), -jnp.inf, F32)
    l0 = jnp.zeros((rows, 1), F32)
    acc0 = jnp.zeros((rows, HEAD_DIM), F32)
    _, l, acc = lax.fori_loop(0, seq // tk, step, (m0, l0, acc0))
    o = acc / l
    for h in range(GQA_GROUP):
        o_ref[:, h * HEAD_DIM:(h + 1) * HEAD_DIM] = o[h * tq:(h + 1) * tq].astype(o_ref.dtype)


def _flash(qkv, *, batch, seq, d, n_kv):
    t_rows = qkv.shape[0]
    tq = _tile(seq, 256)
    tk = _tile(seq, 512)
    nqb = seq // tq
    n_heads = d // HEAD_DIM
    gw = GQA_GROUP * HEAD_DIM
    return pl.pallas_call(
        functools.partial(_flash_kernel, tq=tq, tk=tk, seq=seq),
        out_shape=jax.ShapeDtypeStruct((t_rows, d), BF16),
        grid=(batch, n_kv, nqb),
        in_specs=[
            pl.BlockSpec((tq, gw), lambda b, kv, qi: (b * nqb + qi, kv)),
            pl.BlockSpec((seq, HEAD_DIM), lambda b, kv, qi: (b, n_heads + kv)),
            pl.BlockSpec((seq, HEAD_DIM), lambda b, kv, qi: (b, n_heads + n_kv + kv)),
        ],
        out_specs=pl.BlockSpec((tq, gw), lambda b, kv, qi: (b * nqb + qi, kv)),
        compiler_params=_cparams("arbitrary", "arbitrary", "arbitrary"),
        name="flash_attention",
    )(qkv, qkv, qkv)


def _dft_sizes(seq):
    n = 2 * seq
    n2 = DFT_N2_MAX
    while n // n2 < DFT_N1_MIN:
        n2 //= 2
    return n, n // n2, n2


def _dft_tables(seq):
    n, n1, n2 = _dft_sizes(seq)
    h2 = n2 // 2
    k2 = jnp.arange(h2, dtype=jnp.int32)
    m2 = jnp.arange(h2, dtype=jnp.int32)
    idx = (m2[None, :] * (2 * k2[:, None] + 1)) % (2 * n2)
    th = idx.astype(F32) * (2.0 * math.pi / (2 * n2))
    m1 = jnp.concatenate([jnp.cos(th), -jnp.sin(th)], axis=0)
    a1 = jnp.arange(n1, dtype=jnp.int32)
    k1 = jnp.arange(n1, dtype=jnp.int32)
    freq = 2 * n2 * k1[None, :, None] + 2 * k2[:, None, None] + 1
    idx = (a1[None, None, :] * freq) % (2 * n)
    ph = idx.astype(F32) * (2.0 * math.pi / (2 * n))
    c, s = jnp.cos(ph), jnp.sin(ph)
    gb = jnp.concatenate([jnp.concatenate([c, s], axis=2), jnp.concatenate([-s, c], axis=2)], axis=1)
    return m1, gb


def _filt_kernel(z_ref, w1_ref, b1_ref, w2_ref, b2_ref, fr_ref, w3_ref, dl_ref, o_ref):
    z = z_ref[...]
    fr = fr_ref[...]
    a = jnp.sin(fr * (_dot3(z, w1_ref[...]) + b1_ref[...]))
    a = jnp.sin(fr * (_dot3(a, w2_ref[...]) + b2_ref[...]))
    decay = jnp.exp(-z[:, 0:1] * dl_ref[...]) + HYENA_MOD_SHIFT
    f = _dot3(a, w3_ref[...]) * decay
    row = pl.program_id(0) * z.shape[0] + lax.broadcasted_iota(jnp.int32, (z.shape[0], 1), 0)
    drop = jnp.logical_and(row == 0, pl.program_id(1) % 2 == 1)
    o_ref[...] = jnp.where(drop, 0.0, f)


def _filters(z, w1, b1, w2, b2, fr, w3, deltas):
    seq = z.shape[0]
    d = deltas.shape[1]
    nc = w3.shape[1] // d
    tl = _tile(seq, 256)
    hid = w2.shape[0]
    const = lambda i, c: (0, 0)
    return pl.pallas_call(
        _filt_kernel,
        out_shape=jax.ShapeDtypeStruct((seq, nc * d), F32),
        grid=(seq // tl, nc),
        in_specs=[
            pl.BlockSpec((tl, HYENA_EMB_PAD), lambda i, c: (i, 0)),
            pl.BlockSpec((HYENA_EMB_PAD, hid), const),
            pl.BlockSpec((1, hid), const),
            pl.BlockSpec((hid, hid), const),
            pl.BlockSpec((1, hid), const),
            pl.BlockSpec((1, hid), const),
            pl.BlockSpec((hid, d), lambda i, c: (0, c)),
            pl.BlockSpec((1, d), const),
        ],
        out_specs=pl.BlockSpec((tl, d), lambda i, c: (i, c)),
        compiler_params=_cparams("arbitrary", "arbitrary"),
        name="hyena_filters",
    )(z, w1, b1, w2, b2, fr, w3, deltas)


def _dft1_kernel(x_ref, m_ref, o_ref):
    o_ref[0] = _dot3_const(m_ref[...], x_ref[0])


def _dft1(xv, m1cat, *, tn, n_blocks, col_map):
    b, h2, _ = xv.shape
    n2 = m1cat.shape[0]
    return pl.pallas_call(
        _dft1_kernel,
        out_shape=jax.ShapeDtypeStruct((b, n2, n_blocks * tn), F32),
        grid=(b, n_blocks),
        in_specs=[
            pl.BlockSpec((1, h2, tn), lambda bi, c: (bi, 0, col_map(c))),
            pl.BlockSpec(m1cat.shape, lambda bi, c: (0, 0)),
        ],
        out_specs=pl.BlockSpec((1, n2, tn), lambda bi, c: (bi, 0, c)),
        compiler_params=_cparams("arbitrary", "arbitrary"),
        name="dft_stage1",
    )(xv, m1cat)


def _fspec_kernel(af_ref, ab_ref, g_ref, o_ref, *, kb, n1):
    for kk in range(kb):
        xf = _dot3_const(g_ref[kk], jnp.concatenate([af_ref[0, kk], af_ref[1, kk]], axis=0))
        xb = _dot3_const(g_ref[kk], jnp.concatenate([ab_ref[0, kk], ab_ref[1, kk]], axis=0))
        o_ref[0, 0, kk] = xf[:n1] + xb[:n1]
        o_ref[0, 1, kk] = xf[n1:] - xb[n1:]


def _filter_spectrum(af, gcat, *, d, kb, td):
    _, h2, n1, w = af.shape
    ndt = d // td
    norder = w // (2 * d)
    return pl.pallas_call(
        functools.partial(_fspec_kernel, kb=kb, n1=n1),
        out_shape=jax.ShapeDtypeStruct((norder, 2, h2, n1, d), F32),
        grid=(norder, h2 // kb, ndt),
        in_specs=[
            pl.BlockSpec((2, kb, n1, td), lambda o, k, t: (0, k, 0, (2 * o) * ndt + t)),
            pl.BlockSpec((2, kb, n1, td), lambda o, k, t: (0, k, 0, (2 * o + 1) * ndt + t)),
            pl.BlockSpec((kb, 2 * n1, 6 * n1), lambda o, k, t: (k, 0, 0)),
        ],
        out_specs=pl.BlockSpec((1, 2, kb, n1, td), lambda o, k, t: (o, 0, k, 0, t)),
        compiler_params=_cparams("arbitrary", "arbitrary", "arbitrary"),
        name="hyena_filter_spectrum",
    )(af, af, gcat)


def _cmid_kernel(a_ref, ks_ref, g_ref, gt_ref, o_ref, *, kb, n1):
    for kk in range(kb):
        x = _dot3_const(g_ref[kk], jnp.concatenate([a_ref[0, 0, kk], a_ref[0, 1, kk]], axis=0))
        xr, xi = x[:n1], x[n1:]
        kr, ki = ks_ref[0, 0, kk], ks_ref[0, 1, kk]
        y = jnp.concatenate([xr * kr - xi * ki, xr * ki + xi * kr], axis=0)
        bp = _dot3_const(gt_ref[kk], y)
        o_ref[0, 0, kk] = bp[:n1]
        o_ref[0, 1, kk] = bp[n1:]


def _conv_mid(a, kspec, order, gcat, gtcat, *, kb, td):
    b, _, h2, n1, d = a.shape
    return pl.pallas_call(
        functools.partial(_cmid_kernel, kb=kb, n1=n1),
        out_shape=jax.ShapeDtypeStruct(a.shape, F32),
        grid=(h2 // kb, d // td, b),
        in_specs=[
            pl.BlockSpec((1, 2, kb, n1, td), lambda k, t, bi: (bi, 0, k, 0, t)),
            pl.BlockSpec((1, 2, kb, n1, td), lambda k, t, bi: (order, 0, k, 0, t)),
            pl.BlockSpec((kb, 2 * n1, 6 * n1), lambda k, t, bi: (k, 0, 0)),
            pl.BlockSpec((kb, 2 * n1, 6 * n1), lambda k, t, bi: (k, 0, 0)),
        ],
        out_specs=pl.BlockSpec((1, 2, kb, n1, td), lambda k, t, bi: (bi, 0, k, 0, t)),
        compiler_params=_cparams("arbitrary", "arbitrary", "arbitrary"),
        name="hyena_conv_mid",
    )(a, kspec, gcat, gtcat)


def _idft1_kernel(bp_ref, m_ref, gate_ref, zz_ref, skip_ref, o_ref):
    y = _dot3_const(m_ref[...], bp_ref[0])
    o_ref[0] = (gate_ref[0] * (y + zz_ref[0] * skip_ref[...])).astype(o_ref.dtype)


def _idft1_gate(bp, minvcat, uv, zzv, skip, *, d, n1, gate_col, zz_cols, zz_col, out_dtype):
    b, n2, _ = bp.shape
    h2 = n2 // 2
    return pl.pallas_call(
        _idft1_kernel,
        out_shape=jax.ShapeDtypeStruct((b, h2, n1 * d), out_dtype),
        grid=(b, n1),
        in_specs=[
            pl.BlockSpec((1, n2, d), lambda bi, c: (bi, 0, c)),
            pl.BlockSpec(minvcat.shape, lambda bi, c: (0, 0)),
            pl.BlockSpec((1, h2, d), lambda bi, c: (bi, 0, 3 * c + gate_col)),
            pl.BlockSpec((1, h2, d), lambda bi, c: (bi, 0, zz_cols * c + zz_col)),
            pl.BlockSpec((1, d), lambda bi, c: (0, 0)),
        ],
        out_specs=pl.BlockSpec((1, h2, d), lambda bi, c: (bi, 0, c)),
        compiler_params=_cparams("arbitrary", "arbitrary"),
        name="idft_stage1_gate",
    )(bp, minvcat, uv, zzv, skip)


def _hyena_features(seq, d):
    t = jnp.linspace(0.0, 1.0, seq, dtype=F32)[:, None]
    w = 2.0 * math.pi * jnp.arange(seq, dtype=F32)[:, None] / seq
    f = jnp.linspace(1e-4, HYENA_BANDS - 1, HYENA_BANDS, dtype=F32)[None, :]
    z = jnp.concatenate([t, jnp.cos(f * w), -jnp.sin(f * w)], axis=-1)
    z = jnp.pad(z, ((0, 0), (0, HYENA_EMB_PAD - HYENA_EMB)))
    deltas = jnp.abs(jnp.linspace(HYENA_MIN_DECAY, HYENA_MAX_DECAY, d, dtype=F32))[None, :]
    return z, deltas


def _hyena_mixer(x, mod, gain, p, *, batch, seq):
    t_rows, d = x.shape
    n, n1, n2 = _dft_sizes(seq)
    h2 = n2 // 2
    u = _up(x, mod, gain, p["w_in"], p["b_in"], p["conv_w"], p["conv_b"], seq=seq, sh_idx=0, sc_idx=1,
            glu=False, out_dtype=F32)

    z, deltas = _hyena_features(seq, d)
    filt = _filters(z, p["f_w1"], p["f_b1"], p["f_w2"], p["f_b2"], p["f_freq"], p["f_w3"], deltas)

    m1, gb = _dft_tables(seq)
    m1cat = _cat3(m1)
    minvcat = _cat3(m1.T * (2.0 / n))
    gcat = _cat3(gb)
    gtcat = _cat3(jnp.swapaxes(gb, 1, 2))
    kb = max(1, 256 // n1)
    td = _tile(d, 1024)

    ncf = filt.shape[1]
    af = _dft1(filt.reshape(1, h2, n1 * ncf), m1cat, tn=d, n_blocks=n1 * ncf // d, col_map=lambda c: c)
    kspec = _filter_spectrum(af.reshape(2, h2, n1, ncf), gcat, d=d, kb=kb, td=td)

    uv = u.reshape(batch, h2, n1 * 3 * d)
    zzv, zz_cols = uv, 3
    for o in range(HYENA_ORDER):
        a = _dft1(zzv, m1cat, tn=d, n_blocks=n1, col_map=(lambda c: 3 * c) if o == 0 else (lambda c: c))
        bp = _conv_mid(a.reshape(batch, 2, h2, n1, d), kspec, o, gcat, gtcat, kb=kb, td=td)
        last = o == HYENA_ORDER - 1
        zzv = _idft1_gate(bp.reshape(batch, n2, n1 * d), minvcat, uv, zzv, p["skip"][o:o + 1], d=d, n1=n1,
                          gate_col=1 + o, zz_cols=zz_cols, zz_col=0, out_dtype=BF16 if last else F32)
        zz_cols = 1
    return zzv.reshape(t_rows, d)


def _trunk(x3, modall, row0, p):
    batch, seq, d = x3.shape
    t_rows = batch * seq
    x = x3.reshape(t_rows, d)
    n_kv = d // HEAD_DIM // GQA_GROUP
    depth = modall.shape[0]
    for i in range(depth):
        mod = modall[i, row0:row0 + batch].reshape(batch, 6, d)
        kind, j = i % 3, i // 3
        g_pre, g_post = p["norm_mix_pre"][i:i + 1], p["norm_mix_post"][i:i + 1]
        if kind == 0:
            x = _pool_layer(x, mod, g_pre, g_post, p["pool_w"][j], p["pool_scale"][j:j + 1], seq=seq)
        elif kind == 1:
            hp = {k: v[j] for k, v in p["hy"].items()}
            zz = _hyena_mixer(x, mod, g_pre, hp, batch=batch, seq=seq)
            x = _down(zz, hp["w_out"], x, mod, g_post, seq=seq, gate_idx=2)
        else:
            cos, sin = _rope_tables(seq)
            gains = jnp.stack([p["at_q_gain"][j], p["at_k_gain"][j]], axis=0)
            qkv = _qkv(x, mod, g_pre, p["at_w_qkv"][j], gains, cos, sin, seq=seq, n_kv=n_kv)
            o = _flash(qkv, batch=batch, seq=seq, d=d, n_kv=n_kv)
            x = _down(o, p["at_w_o"][j], x, mod, g_post, seq=seq, gate_idx=2)
        g = _up(x, mod, p["norm_ffn_pre"][i:i + 1], p["ffn_w_up"][i], None, p["ffn_conv_w"][i],
                p["ffn_conv_b"][i:i + 1], seq=seq, sh_idx=3, sc_idx=4, glu=True, out_dtype=BF16)
        x = _down(g, p["ffn_w_down"][i], x, mod, p["norm_ffn_post"][i:i + 1], seq=seq, gate_idx=5)
    return x.reshape(batch, seq, d)


def kernel(x_prompt, x_sample, c_prompt, c_sample, mod_w, mod_b, norm_mix_pre, norm_mix_post, norm_ffn_pre,
           norm_ffn_post, ffn_w_up, ffn_conv_w, ffn_conv_b, ffn_w_down, pool_w, pool_scale, hy_w_in, hy_b_in,
           hy_conv_w, hy_conv_b, hy_f_w1, hy_f_b1, hy_f_w2, hy_f_b2, hy_f_w3, hy_f_freq, hy_skip, hy_w_out,
           at_w_qkv, at_q_gain, at_k_gain, at_w_o):
    nb = c_prompt.shape[0]
    ns = c_sample.shape[0]
    assert nb + ns <= MOD_ROWS
    c8 = jnp.concatenate([c_prompt, c_sample, jnp.zeros((MOD_ROWS - nb - ns, c_prompt.shape[1]), F32)], axis=0)
    modall = _modulation(c8, mod_w, mod_b)
    row = lambda v: v[:, None, :]
    p = dict(
        norm_mix_pre=norm_mix_pre, norm_mix_post=norm_mix_post, norm_ffn_pre=norm_ffn_pre,
        norm_ffn_post=norm_ffn_post,
        ffn_w_up=ffn_w_up.astype(BF16), ffn_conv_w=ffn_conv_w, ffn_conv_b=ffn_conv_b,
        ffn_w_down=ffn_w_down.astype(BF16),
        pool_w=pool_w.astype(BF16), pool_scale=pool_scale,
        hy=dict(
            w_in=hy_w_in.astype(BF16), b_in=row(hy_b_in), conv_w=hy_conv_w, conv_b=row(hy_conv_b),
            f_w1=jnp.pad(hy_f_w1, ((0, 0), (0, HYENA_EMB_PAD - HYENA_EMB), (0, 0))), f_b1=row(hy_f_b1),
            f_w2=hy_f_w2, f_b2=row(hy_f_b2), f_w3=hy_f_w3, f_freq=row(hy_f_freq), skip=hy_skip,
            w_out=hy_w_out.astype(BF16)),
        at_w_qkv=at_w_qkv.astype(BF16), at_q_gain=at_q_gain, at_k_gain=at_k_gain, at_w_o=at_w_o.astype(BF16),
    )
    y_prompt = _trunk(x_prompt, modall, 0, p)
    y_sample = _trunk(x_sample, modall, nb, p)
    return (y_prompt, y_sample)
```

```python
import functools
import math

import jax
import jax.numpy as jnp
from jax import lax
from jax.experimental import pallas as pl
from jax.experimental.pallas import tpu as pltpu

EPS = 1e-6
HEAD_DIM = 128
GQA_GROUP = 4
GRID_W = 64
ROPE_THETA = 10000.0
POOL_WINDOWS = (2, 4, 8, 16)
HYENA_ORDER = 2
HYENA_EMB = 33
HYENA_BANDS = (HYENA_EMB - 1) // 2
HYENA_EMB_PAD = 64
HYENA_FAST_DECAY = 0.3
HYENA_SLOW_DECAY = 1.5
HYENA_TARGET = 1e-2
HYENA_MOD_SHIFT = 0.05
HYENA_MIN_DECAY = math.log(HYENA_TARGET) / HYENA_SLOW_DECAY
HYENA_MAX_DECAY = math.log(HYENA_TARGET) / HYENA_FAST_DECAY

HALO = 16
MOD_ROWS = 8
DFT_N2_MAX = 512
DFT_N1_MIN = 16
SUBLANES = 8
LANES = 128
FLASH_TK = 512
ONES_ROWS = 16
LOG2E = 1.4426950408889634
VMEM_LIMIT = 52 * 1024 * 1024

F32 = jnp.float32
BF16 = jnp.bfloat16


def _cparams(*sem):
    return pltpu.CompilerParams(dimension_semantics=sem, vmem_limit_bytes=VMEM_LIMIT)


def _tile(dim, pref):
    t = min(dim, pref)
    while dim % t:
        t //= 2
    return t


def _lane_tile(dim, cap):
    units = dim // LANES
    best = max(u for u in range(1, units + 1) if units % u == 0 and u * LANES <= max(cap, LANES))
    return best * LANES


def _split3(x):
    hi = x.astype(BF16)
    lo = (x - hi.astype(F32)).astype(BF16)
    return hi, lo


def _dot3(a, b):
    ah, al = _split3(a)
    bh, bl = _split3(b)
    d = functools.partial(jnp.dot, preferred_element_type=F32)
    return d(ah, bh) + (d(ah, bl) + d(al, bh))


def _dot3_const(mcat, x, axis=0):
    xh, xl = _split3(x)
    return jnp.dot(mcat, jnp.concatenate([xh, xl, xh], axis=axis), preferred_element_type=F32)


def _cat3(m):
    mh, ml = _split3(m)
    return jnp.concatenate([mh, mh, ml], axis=-1)


def _modnorm(xv, gain, scale1p, shift):
    ms = jnp.mean(xv * xv, axis=-1, keepdims=True)
    return xv * lax.rsqrt(ms + EPS) * gain * scale1p + shift


def _halo_maps(tm, t_rows):
    r = tm // HALO
    last = t_rows // HALO - 1
    prev = lambda i, *_: (jnp.maximum(i * r - 1, 0), 0)
    nxt = lambda i, *_: (jnp.minimum((i + 1) * r, last), 0)
    return prev, nxt


def _mod_kernel(c_ref, w_ref, b_ref, o_ref):
    c = c_ref[...]
    a = (c / (1.0 + jnp.exp(-c))).astype(BF16)
    o_ref[0] = jnp.dot(a, w_ref[0].astype(BF16), preferred_element_type=F32) + b_ref[0]


def _modulation(c8, mod_w, mod_b):
    depth, d, n = mod_w.shape
    tn = _tile(n, 1024)
    return pl.pallas_call(
        _mod_kernel,
        out_shape=jax.ShapeDtypeStruct((depth, MOD_ROWS, n), F32),
        grid=(depth, n // tn),
        in_specs=[
            pl.BlockSpec((MOD_ROWS, d), lambda l, j: (0, 0)),
            pl.BlockSpec((1, d, tn), lambda l, j: (l, 0, j)),
            pl.BlockSpec((1, 1, tn), lambda l, j: (l, 0, j)),
        ],
        out_specs=pl.BlockSpec((1, MOD_ROWS, tn), lambda l, j: (l, 0, j)),
        compiler_params=_cparams("arbitrary", "arbitrary"),
        name="modulation",
    )(c8, mod_w, mod_b.reshape(depth, 1, n))


def _fill_h(h_scr, x_ref, xp_ref, xn_ref, mod_ref, g_ref, sh_idx, sc_idx, tm):
    m = mod_ref[0]
    shift = m[sh_idx:sh_idx + 1]
    scale1p = 1.0 + m[sc_idx:sc_idx + 1]
    gain = g_ref[...]
    h_scr[0:HALO] = _modnorm(xp_ref[...], gain, scale1p, shift).astype(h_scr.dtype)
    h_scr[HALO:HALO + tm] = _modnorm(x_ref[...], gain, scale1p, shift).astype(h_scr.dtype)
    h_scr[HALO + tm:] = _modnorm(xn_ref[...], gain, scale1p, shift).astype(h_scr.dtype)


def _edge_keep(tps):
    i = pl.program_id(0) % tps
    return (i != 0).astype(F32), (i != tps - 1).astype(F32)


def _conv3_rows(a, keep_first, keep_last, cw, cb, tm):
    rows = tm + 2 * HALO
    a = jnp.concatenate([a[0:HALO] * keep_first, a[HALO:HALO + tm], a[HALO + tm:] * keep_last], axis=0)
    prev = pltpu.roll(a, 1, 0)
    nxt = pltpu.roll(a, rows - 1, 0)
    c = prev * cw[0:1] + a * cw[1:2] + nxt * cw[2:3] + cb
    return c[HALO:HALO + tm]


def _gelu_tanh(x):
    return 0.5 * x * (1.0 + jnp.tanh(0.7978845608028654 * (x + 0.044715 * (x * x * x))))


def _up_kernel(*refs, sh_idx, sc_idx, tm, tps, glu, has_bias):
    x_ref, xp_ref, xn_ref, mod_ref, g_ref, w_ref = refs[:6]
    k = 6
    wb_ref = b_ref = None
    if glu:
        wb_ref = refs[k]
        k += 1
    if has_bias:
        b_ref = refs[k]
        k += 1
    cw_ref, cb_ref, o_ref, h_scr = refs[k:k + 4]

    @pl.when(pl.program_id(1) == 0)
    def _():
        _fill_h(h_scr, x_ref, xp_ref, xn_ref, mod_ref, g_ref, sh_idx, sc_idx, tm)

    a = jnp.dot(h_scr[...], w_ref[...], preferred_element_type=F32)
    if has_bias:
        a = a + b_ref[...]
    keep_first, keep_last = _edge_keep(tps)
    c = _conv3_rows(a, keep_first, keep_last, cw_ref[...], cb_ref[...], tm)
    if glu:
        b = jnp.dot(h_scr[HALO:HALO + tm], wb_ref[...], preferred_element_type=F32)
        c = _gelu_tanh(c) * b
    o_ref[...] = c.astype(o_ref.dtype)


def _up(x, mod, gain, w, bias, cw, cb, *, seq, sh_idx, sc_idx, glu, out_dtype):
    t_rows, d = x.shape
    n_out = cw.shape[-1]
    tm = _tile(seq, 1024)
    tn = _tile(n_out, 512)
    tps = seq // tm
    nj = n_out // tn
    prev, nxt = _halo_maps(tm, t_rows)
    in_specs = [
        pl.BlockSpec((tm, d), lambda i, j: (i, 0)),
        pl.BlockSpec((HALO, d), prev),
        pl.BlockSpec((HALO, d), nxt),
        pl.BlockSpec((1, 6, d), lambda i, j: (i // tps, 0, 0)),
        pl.BlockSpec((1, d), lambda i, j: (0, 0)),
        pl.BlockSpec((d, tn), lambda i, j: (0, j)),
    ]
    args = [x, x, x, mod, gain, w]
    if glu:
        in_specs.append(pl.BlockSpec((d, tn), lambda i, j: (0, j + nj)))
        args.append(w)
    if bias is not None:
        in_specs.append(pl.BlockSpec((1, tn), lambda i, j: (0, j)))
        args.append(bias)
    in_specs += [pl.BlockSpec((3, tn), lambda i, j: (0, j)), pl.BlockSpec((1, tn), lambda i, j: (0, j))]
    args += [cw, cb]
    return pl.pallas_call(
        functools.partial(_up_kernel, sh_idx=sh_idx, sc_idx=sc_idx, tm=tm, tps=tps, glu=glu,
                          has_bias=bias is not None),
        out_shape=jax.ShapeDtypeStruct((t_rows, n_out), out_dtype),
        grid=(t_rows // tm, nj),
        in_specs=in_specs,
        out_specs=pl.BlockSpec((tm, tn), lambda i, j: (i, j)),
        scratch_shapes=[pltpu.VMEM((tm + 2 * HALO, d), BF16)],
        compiler_params=_cparams("arbitrary", "arbitrary"),
        name="up_glu" if glu else "up_conv",
    )(*args)


def _down_kernel(g_ref, w_ref, x_ref, mod_ref, gain_ref, o_ref, acc_ref, *, gate_idx, nk):
    k = pl.program_id(1)
    part = jnp.dot(g_ref[...].astype(w_ref.dtype), w_ref[...], preferred_element_type=F32)

    @pl.when(k == 0)
    def _():
        acc_ref[...] = part

    @pl.when(k > 0)
    def _():
        acc_ref[...] += part

    @pl.when(k == nk - 1)
    def _():
        f = acc_ref[...]
        ms = jnp.mean(f * f, axis=-1, keepdims=True)
        y = f * lax.rsqrt(ms + EPS) * gain_ref[...]
        gate = mod_ref[0][gate_idx:gate_idx + 1]
        o_ref[...] = x_ref[...] + gate * y


def _down(g, w, x, mod, gain, *, seq, gate_idx):
    t_rows, kdim = g.shape
    d = w.shape[1]
    tm = _tile(seq, 512)
    tk = _lane_tile(kdim, 2048)
    tps = seq // tm
    nk = kdim // tk
    return pl.pallas_call(
        functools.partial(_down_kernel, gate_idx=gate_idx, nk=nk),
        out_shape=jax.ShapeDtypeStruct((t_rows, d), F32),
        grid=(t_rows // tm, nk),
        in_specs=[
            pl.BlockSpec((tm, tk), lambda i, k: (i, k)),
            pl.BlockSpec((tk, d), lambda i, k: (k, 0)),
            pl.BlockSpec((tm, d), lambda i, k: (i, 0)),
            pl.BlockSpec((1, 6, d), lambda i, k: (i // tps, 0, 0)),
            pl.BlockSpec((1, d), lambda i, k: (0, 0)),
        ],
        out_specs=pl.BlockSpec((tm, d), lambda i, k: (i, 0)),
        scratch_shapes=[pltpu.VMEM((tm, d), F32)],
        compiler_params=_cparams("arbitrary", "arbitrary"),
        name="down",
    )(g, w, x, mod, gain)


def _pool_kernel(x_ref, xp_ref, xn_ref, mod_ref, gpre_ref, gpost_ref, pw_ref, ps_ref, o_ref, h_scr,
                 *, tm, tps, seq):
    _fill_h(h_scr, x_ref, xp_ref, xn_ref, mod_ref, gpre_ref, 0, 1, tm)
    keep_first, keep_last = _edge_keep(tps)
    rows = tm + 2 * HALO
    d = x_ref.shape[1]
    cg = d // len(POOL_WINDOWS)
    pos = (pl.program_id(0) % tps) * tm + lax.broadcasted_iota(jnp.int32, (tm, 1), 0)
    ys = []
    ssq = jnp.zeros((tm, 1), F32)
    for g, win in enumerate(POOL_WINDOWS):
        half = win // 2
        sl = slice(g * cg, (g + 1) * cg)
        hg = jnp.concatenate([h_scr[0:HALO, sl] * keep_first, h_scr[HALO:HALO + tm, sl],
                              h_scr[HALO + tm:, sl] * keep_last], axis=0)
        p = hg
        s = 1
        while s < win:
            p = p + pltpu.roll(p, s, 0)
            s *= 2
        if half > 1:
            p = pltpu.roll(p, rows - (half - 1), 0)
        lo = jnp.maximum(pos - half, 0)
        hi = jnp.minimum(pos + (half - 1), seq - 1)
        cnt = (hi - lo + 1).astype(F32)
        pooled = p[HALO:HALO + tm] / cnt - hg[HALO:HALO + tm]
        y = jnp.dot(pooled.astype(BF16), pw_ref[g], preferred_element_type=F32) * ps_ref[:, sl]
        ssq = ssq + jnp.sum(y * y, axis=-1, keepdims=True)
        ys.append(y)
    inv = lax.rsqrt(ssq / d + EPS)
    gate = mod_ref[0][2:3]
    for g in range(len(POOL_WINDOWS)):
        sl = slice(g * cg, (g + 1) * cg)
        o_ref[:, sl] = x_ref[:, sl] + gate[:, sl] * (ys[g] * inv * gpost_ref[:, sl])


def _pool_layer(x, mod, gpre, gpost, pw, ps, *, seq):
    t_rows, d = x.shape
    tm = _tile(seq, 256)
    tps = seq // tm
    prev, nxt = _halo_maps(tm, t_rows)
    return pl.pallas_call(
        functools.partial(_pool_kernel, tm=tm, tps=tps, seq=seq),
        out_shape=jax.ShapeDtypeStruct((t_rows, d), F32),
        grid=(t_rows // tm,),
        in_specs=[
            pl.BlockSpec((tm, d), lambda i: (i, 0)),
            pl.BlockSpec((HALO, d), prev),
            pl.BlockSpec((HALO, d), nxt),
            pl.BlockSpec((1, 6, d), lambda i: (i // tps, 0, 0)),
            pl.BlockSpec((1, d), lambda i: (0, 0)),
            pl.BlockSpec((1, d), lambda i: (0, 0)),
            pl.BlockSpec(pw.shape, lambda i: (0, 0, 0)),
            pl.BlockSpec((1, d), lambda i: (0, 0)),
        ],
        out_specs=pl.BlockSpec((tm, d), lambda i: (i, 0)),
        scratch_shapes=[pltpu.VMEM((tm + 2 * HALO, d), F32)],
        compiler_params=_cparams("arbitrary"),
        name="pool_layer",
    )(x, x, x, mod, gpre, gpost, pw, ps)


def _rope_tables(seq):
    pos = jnp.arange(seq, dtype=jnp.int32)
    row = (pos // GRID_W).astype(F32)[:, None]
    col = (pos % GRID_W).astype(F32)[:, None]
    axis_dim = HEAD_DIM // 2
    inv = ROPE_THETA ** (-jnp.arange(0, axis_dim, 2, dtype=F32) / axis_dim)
    ar, ac = row * inv, col * inv
    cos = jnp.concatenate([jnp.cos(ar), jnp.cos(ar), jnp.cos(ac), jnp.cos(ac)], axis=-1)
    sin = jnp.concatenate([-jnp.sin(ar), jnp.sin(ar), -jnp.sin(ac), jnp.sin(ac)], axis=-1)
    return cos, sin


def _qkv_kernel(x_ref, mod_ref, g_ref, w_ref, hg_ref, cos_ref, sin_ref, q_ref, k_ref, vt_ref, h_scr, *, n_q_tiles):
    j = pl.program_id(1)

    @pl.when(j == 0)
    def _():
        m = mod_ref[0]
        h_scr[...] = _modnorm(x_ref[...], g_ref[...], 1.0 + m[1:2], m[0:1]).astype(BF16)

    a = jnp.dot(h_scr[...], w_ref[...], preferred_element_type=F32)

    def norm_rope(gain, out_scale, o_ref):
        cos = cos_ref[...]
        sin = sin_ref[...]
        quarter = HEAD_DIM // 4
        lane = lax.broadcasted_iota(jnp.int32, (1, HEAD_DIM), 1)
        low = (lane % (2 * quarter)) < quarter
        for h in range(a.shape[1] // HEAD_DIM):
            sl = slice(h * HEAD_DIM, (h + 1) * HEAD_DIM)
            v = a[:, sl]
            ms = jnp.mean(v * v, axis=-1, keepdims=True)
            v = v * lax.rsqrt(ms + EPS) * gain
            swapped = jnp.where(low, pltpu.roll(v, HEAD_DIM - quarter, 1), pltpu.roll(v, quarter, 1))
            o_ref[:, sl] = ((v * cos + swapped * sin) * out_scale).astype(o_ref.dtype)

    @pl.when(j < n_q_tiles)
    def _():
        norm_rope(hg_ref[0:1], HEAD_DIM ** -0.5 * LOG2E, q_ref)

    @pl.when(j == n_q_tiles)
    def _():
        norm_rope(hg_ref[1:2], 1.0, k_ref)

    @pl.when(j == n_q_tiles + 1)
    def _():
        for c in range(vt_ref.shape[0]):
            vt_ref[c] = a[c * FLASH_TK:(c + 1) * FLASH_TK].T.astype(vt_ref.dtype)


def _qkv(x, mod, gain, w, head_gains, cos, sin, *, seq, n_kv):
    t_rows, d = x.shape
    n_out = w.shape[1]
    tm = _tile(seq, 512)
    tn = HEAD_DIM * n_kv
    tps = seq // tm
    n_q_tiles = d // tn
    assert tm % FLASH_TK == 0 and n_out == d + 2 * tn
    return pl.pallas_call(
        functools.partial(_qkv_kernel, n_q_tiles=n_q_tiles),
        out_shape=(jax.ShapeDtypeStruct((t_rows, d), BF16), jax.ShapeDtypeStruct((t_rows, tn), BF16),
                   jax.ShapeDtypeStruct((t_rows // FLASH_TK, tn, FLASH_TK), BF16)),
        grid=(t_rows // tm, n_out // tn),
        in_specs=[
            pl.BlockSpec((tm, d), lambda i, j: (i, 0)),
            pl.BlockSpec((1, 6, d), lambda i, j: (i // tps, 0, 0)),
            pl.BlockSpec((1, d), lambda i, j: (0, 0)),
            pl.BlockSpec((d, tn), lambda i, j: (0, j)),
            pl.BlockSpec((2, HEAD_DIM), lambda i, j: (0, 0)),
            pl.BlockSpec((tm, HEAD_DIM), lambda i, j: (i % tps, 0)),
            pl.BlockSpec((tm, HEAD_DIM), lambda i, j: (i % tps, 0)),
        ],
        out_specs=(
            pl.BlockSpec((tm, tn), lambda i, j: (i, jnp.minimum(j, n_q_tiles - 1))),
            pl.BlockSpec((tm, tn), lambda i, j: (i, 0)),
            pl.BlockSpec((tm // FLASH_TK, tn, FLASH_TK), lambda i, j: (i, 0, 0)),
        ),
        scratch_shapes=[pltpu.VMEM((tm, d), BF16)],
        compiler_params=_cparams("arbitrary", "arbitrary"),
        name="qkv_rope",
    )(x, mod, gain, w, head_gains, cos, sin)


def _flash_kernel(q_ref, k_ref, vt_ref, o_ref, acc_ref, s_ref, *, tq, nk):
    q = q_ref[...]
    qs = jnp.concatenate([q[:, h * HEAD_DIM:(h + 1) * HEAD_DIM] for h in range(GQA_GROUP)], axis=0)
    ones = jnp.ones((ONES_ROWS, FLASH_TK), BF16)
    acc_ref[...] = jnp.zeros_like(acc_ref)

    def scores(kk, slot):
        start = pl.multiple_of(kk * FLASH_TK, FLASH_TK)
        s_ref[slot] = lax.dot_general(k_ref[pl.ds(start, FLASH_TK), :], qs, (((1,), (1,)), ((), ())),
                                      preferred_element_type=F32)

    def accumulate(kk, slot, m):
        st = s_ref[slot]
        m_new = jnp.maximum(m, jnp.max(st, axis=0, keepdims=True))
        alpha = jnp.exp2(m - m_new)
        pt = jnp.exp2(st - m_new).astype(BF16)
        va = jnp.concatenate([vt_ref[kk], ones], axis=0)
        acc_ref[...] = alpha * acc_ref[...] + jnp.dot(va, pt, preferred_element_type=F32)
        return m_new

    scores(0, 0)

    def pair(i, m):
        k0 = 2 * i
        scores(k0 + 1, 1)
        m = accumulate(k0, 0, m)
        scores(jnp.minimum(k0 + 2, nk - 1), 0)
        return accumulate(k0 + 1, 1, m)

    lax.fori_loop(0, nk // 2, pair, jnp.full((1, GQA_GROUP * tq), -jnp.inf, F32))
    acc = acc_ref[...]
    ot = acc[:HEAD_DIM] / acc[HEAD_DIM:HEAD_DIM + 1]
    for h in range(GQA_GROUP):
        o_ref[:, h * HEAD_DIM:(h + 1) * HEAD_DIM] = ot[:, h * tq:(h + 1) * tq].T.astype(o_ref.dtype)


def _flash(q, k, vt, *, batch, seq, n_kv):
    t_rows, d = q.shape
    tq = _tile(seq, 256)
    nqb = seq // tq
    nk = seq // FLASH_TK
    assert nk % 2 == 0
    gw = GQA_GROUP * HEAD_DIM
    return pl.pallas_call(
        functools.partial(_flash_kernel, tq=tq, nk=nk),
        out_shape=jax.ShapeDtypeStruct((t_rows, d), BF16),
        grid=(batch, n_kv, nqb),
        in_specs=[
            pl.BlockSpec((tq, gw), lambda b, kv, qi: (b * nqb + qi, kv)),
            pl.BlockSpec((seq, HEAD_DIM), lambda b, kv, qi: (b, kv)),
            pl.BlockSpec((nk, HEAD_DIM, FLASH_TK), lambda b, kv, qi: (b, kv, 0)),
        ],
        out_specs=pl.BlockSpec((tq, gw), lambda b, kv, qi: (b * nqb + qi, kv)),
        scratch_shapes=[pltpu.VMEM((HEAD_DIM + ONES_ROWS, GQA_GROUP * tq), F32),
                        pltpu.VMEM((2, FLASH_TK, GQA_GROUP * tq), F32)],
        compiler_params=_cparams("arbitrary", "arbitrary", "arbitrary"),
        name="flash_attention",
    )(q, k, vt)


def _dft_sizes(seq):
    n = 2 * seq
    n2 = DFT_N2_MAX
    while n // n2 < DFT_N1_MIN:
        n2 //= 2
    return n, n // n2, n2


def _dft_tables(seq):
    n, n1, n2 = _dft_sizes(seq)
    h2 = n2 // 2
    k2 = jnp.arange(h2, dtype=jnp.int32)
    m2 = jnp.arange(h2, dtype=jnp.int32)
    idx = (m2[None, :] * (2 * k2[:, None] + 1)) % (2 * n2)
    th = idx.astype(F32) * (2.0 * math.pi / (2 * n2))
    m1 = jnp.concatenate([jnp.cos(th), -jnp.sin(th)], axis=0)
    a1 = jnp.arange(n1, dtype=jnp.int32)
    k1 = jnp.arange(n1, dtype=jnp.int32)
    freq = 2 * n2 * k1[None, :, None] + 2 * k2[:, None, None] + 1
    idx = (a1[None, None, :] * freq) % (2 * n)
    ph = idx.astype(F32) * (2.0 * math.pi / (2 * n))
    c, s = jnp.cos(ph), jnp.sin(ph)
    gb = jnp.concatenate([jnp.concatenate([c, s], axis=2), jnp.concatenate([-s, c], axis=2)], axis=1)
    return m1, gb


def _filt_kernel(z_ref, w1_ref, b1_ref, w2_ref, b2_ref, fr_ref, w3_ref, dl_ref, o_ref):
    z = z_ref[...]
    fr = fr_ref[...]
    a = jnp.sin(fr * (_dot3(z, w1_ref[...]) + b1_ref[...]))
    a = jnp.sin(fr * (_dot3(a, w2_ref[...]) + b2_ref[...]))
    decay = jnp.exp(-z[:, 0:1] * dl_ref[...]) + HYENA_MOD_SHIFT
    d = decay.shape[1]
    row = pl.program_id(0) * z.shape[0] + lax.broadcasted_iota(jnp.int32, (z.shape[0], 1), 0)
    for c in range(w3_ref.shape[1] // d):
        f = _dot3(a, w3_ref[:, c * d:(c + 1) * d]) * decay
        o_ref[:, c * d:(c + 1) * d] = jnp.where(row == 0, 0.0, f) if c % 2 == 1 else f


def _filters(z, w1, b1, w2, b2, fr, w3, deltas):
    seq = z.shape[0]
    d = deltas.shape[1]
    tl = _tile(seq, 128)
    hid = w2.shape[0]
    const = lambda i: (0, 0)
    return pl.pallas_call(
        _filt_kernel,
        out_shape=jax.ShapeDtypeStruct((seq, w3.shape[1]), F32),
        grid=(seq // tl,),
        in_specs=[
            pl.BlockSpec((tl, HYENA_EMB_PAD), lambda i: (i, 0)),
            pl.BlockSpec((HYENA_EMB_PAD, hid), const),
            pl.BlockSpec((1, hid), const),
            pl.BlockSpec((hid, hid), const),
            pl.BlockSpec((1, hid), const),
            pl.BlockSpec((1, hid), const),
            pl.BlockSpec(w3.shape, const),
            pl.BlockSpec((1, d), const),
        ],
        out_specs=pl.BlockSpec((tl, w3.shape[1]), lambda i: (i, 0)),
        compiler_params=_cparams("arbitrary"),
        name="hyena_filters",
    )(z, w1, b1, w2, b2, fr, w3, deltas)


def _dft1_kernel(x_ref, m_ref, o_ref, y_scr):
    xt = pltpu.einshape("hjd->jhd", x_ref[0])
    m = m_ref[...]
    for j in range(SUBLANES):
        y_scr[j] = _dot3_const(m, xt[j])
    yt = pltpu.einshape("jnd->njd", y_scr[...])
    h2 = yt.shape[0] // 2
    o_ref[0, 0] = yt[:h2]
    o_ref[0, 1] = yt[h2:]


def _dft1(x4, m1cat, *, td, n_blocks, col_map):
    b, h2, n1, _ = x4.shape
    n2 = m1cat.shape[0]
    return pl.pallas_call(
        _dft1_kernel,
        out_shape=jax.ShapeDtypeStruct((b, 2, h2, n1, n_blocks * td), F32),
        grid=(b, n1 // SUBLANES, n_blocks),
        in_specs=[
            pl.BlockSpec((1, h2, SUBLANES, td), lambda bi, g, c: (bi, 0, g, col_map(c))),
            pl.BlockSpec(m1cat.shape, lambda bi, g, c: (0, 0)),
        ],
        out_specs=pl.BlockSpec((1, 2, h2, SUBLANES, td), lambda bi, g, c: (bi, 0, 0, g, c)),
        scratch_shapes=[pltpu.VMEM((SUBLANES, n2, td), F32)],
        compiler_params=_cparams("arbitrary", "arbitrary", "arbitrary"),
        name="dft_stage1",
    )(x4, m1cat)


def _fspec_kernel(af_ref, ab_ref, g_ref, o_ref, *, kb, n1):
    for kk in range(kb):
        xf = _dot3_const(g_ref[kk], jnp.concatenate([af_ref[0, kk], af_ref[1, kk]], axis=0))
        xb = _dot3_const(g_ref[kk], jnp.concatenate([ab_ref[0, kk], ab_ref[1, kk]], axis=0))
        o_ref[0, 0, kk] = xf[:n1] + xb[:n1]
        o_ref[0, 1, kk] = xf[n1:] - xb[n1:]


def _filter_spectrum(af, gcat, *, d, kb, td):
    _, h2, n1, w = af.shape
    ndt = d // td
    norder = w // (2 * d)
    return pl.pallas_call(
        functools.partial(_fspec_kernel, kb=kb, n1=n1),
        out_shape=jax.ShapeDtypeStruct((norder, 2, h2, n1, d), F32),
        grid=(norder, h2 // kb, ndt),
        in_specs=[
            pl.BlockSpec((2, kb, n1, td), lambda o, k, t: (0, k, 0, (2 * o) * ndt + t)),
            pl.BlockSpec((2, kb, n1, td), lambda o, k, t: (0, k, 0, (2 * o + 1) * ndt + t)),
            pl.BlockSpec((kb, 2 * n1, 6 * n1), lambda o, k, t: (k, 0, 0)),
        ],
        out_specs=pl.BlockSpec((1, 2, kb, n1, td), lambda o, k, t: (o, 0, k, 0, t)),
        compiler_params=_cparams("arbitrary", "arbitrary", "arbitrary"),
        name="hyena_filter_spectrum",
    )(af, af, gcat)


def _cmid_kernel(a_ref, ks_ref, g_ref, gt_ref, o_ref, *, kb, n1):
    for kk in range(kb):
        x = _dot3_const(g_ref[kk], jnp.concatenate([a_ref[0, 0, kk], a_ref[0, 1, kk]], axis=0))
        xr, xi = x[:n1], x[n1:]
        kr, ki = ks_ref[0, 0, kk], ks_ref[0, 1, kk]
        y = jnp.concatenate([xr * kr - xi * ki, xr * ki + xi * kr], axis=0)
        bp = _dot3_const(gt_ref[kk], y)
        o_ref[0, 0, kk] = bp[:n1]
        o_ref[0, 1, kk] = bp[n1:]


def _conv_mid(a, kspec, order, gcat, gtcat, *, kb, td):
    b, _, h2, n1, d = a.shape
    return pl.pallas_call(
        functools.partial(_cmid_kernel, kb=kb, n1=n1),
        out_shape=jax.ShapeDtypeStruct(a.shape, F32),
        grid=(h2 // kb, d // td, b),
        in_specs=[
            pl.BlockSpec((1, 2, kb, n1, td), lambda k, t, bi: (bi, 0, k, 0, t)),
            pl.BlockSpec((1, 2, kb, n1, td), lambda k, t, bi: (order, 0, k, 0, t)),
            pl.BlockSpec((kb, 2 * n1, 6 * n1), lambda k, t, bi: (k, 0, 0)),
            pl.BlockSpec((kb, 2 * n1, 6 * n1), lambda k, t, bi: (k, 0, 0)),
        ],
        out_specs=pl.BlockSpec((1, 2, kb, n1, td), lambda k, t, bi: (bi, 0, k, 0, t)),
        compiler_params=_cparams("arbitrary", "arbitrary", "arbitrary"),
        name="hyena_conv_mid",
    )(a, kspec, gcat, gtcat)


def _idft1_kernel(bp_ref, m_ref, gate_ref, zz_ref, skip_ref, o_ref, y_scr):
    bp = jnp.concatenate([bp_ref[0, 0], bp_ref[0, 1]], axis=0)
    bt = pltpu.einshape("njd->jnd", bp)
    m = m_ref[...]
    for j in range(SUBLANES):
        y_scr[j] = _dot3_const(m, bt[j])
    y = pltpu.einshape("jhd->hjd", y_scr[...])
    o_ref[0] = (gate_ref[0] * (y + zz_ref[0] * skip_ref[...])).astype(o_ref.dtype)


def _idft1_gate(bp, minvcat, u4, zz4, skip, *, td, gate_part, zz_part):
    b, _, h2, n1, d = bp.shape
    nt = d // td
    return pl.pallas_call(
        _idft1_kernel,
        out_shape=jax.ShapeDtypeStruct((b, h2, n1, d), F32),
        grid=(b, n1 // SUBLANES, nt),
        in_specs=[
            pl.BlockSpec((1, 2, h2, SUBLANES, td), lambda bi, g, c: (bi, 0, 0, g, c)),
            pl.BlockSpec(minvcat.shape, lambda bi, g, c: (0, 0)),
            pl.BlockSpec((1, h2, SUBLANES, td), lambda bi, g, c: (bi, 0, g, gate_part * nt + c)),
            pl.BlockSpec((1, h2, SUBLANES, td), lambda bi, g, c: (bi, 0, g, zz_part * nt + c)),
            pl.BlockSpec((1, td), lambda bi, g, c: (0, c)),
        ],
        out_specs=pl.BlockSpec((1, h2, SUBLANES, td), lambda bi, g, c: (bi, 0, g, c)),
        scratch_shapes=[pltpu.VMEM((SUBLANES, h2, td), F32)],
        compiler_params=_cparams("arbitrary", "arbitrary", "arbitrary"),
        name="idft_stage1_gate",
    )(bp, minvcat, u4, zz4, skip)


def _hyena_features(seq, d):
    t = jnp.linspace(0.0, 1.0, seq, dtype=F32)[:, None]
    w = 2.0 * math.pi * jnp.arange(seq, dtype=F32)[:, None] / seq
    f = jnp.linspace(1e-4, HYENA_BANDS - 1, HYENA_BANDS, dtype=F32)[None, :]
    z = jnp.concatenate([t, jnp.cos(f * w), -jnp.sin(f * w)], axis=-1)
    z = jnp.pad(z, ((0, 0), (0, HYENA_EMB_PAD - HYENA_EMB)))
    deltas = jnp.abs(jnp.linspace(HYENA_MIN_DECAY, HYENA_MAX_DECAY, d, dtype=F32))[None, :]
    return z, deltas


def _hyena_mixer(x, mod, gain, p, *, batch, seq):
    t_rows, d = x.shape
    n, n1, n2 = _dft_sizes(seq)
    h2 = n2 // 2
    u = _up(x, mod, gain, p["w_in"], p["b_in"], p["conv_w"], p["conv_b"], seq=seq, sh_idx=0, sc_idx=1,
            glu=False, out_dtype=F32)

    z, deltas = _hyena_features(seq, d)
    filt = _filters(z, p["f_w1"], p["f_b1"], p["f_w2"], p["f_b2"], p["f_freq"], p["f_w3"], deltas)

    m1, gb = _dft_tables(seq)
    m1cat = _cat3(m1)
    minvcat = _cat3(m1.T * (2.0 / n))
    gcat = _cat3(gb)
    gtcat = _cat3(jnp.swapaxes(gb, 1, 2))
    kb = max(1, 256 // n1)
    td = _tile(d, 1024)
    ts = _tile(d, 256)
    nts = d // ts

    ncf = filt.shape[1]
    af = _dft1(filt.reshape(1, h2, n1, ncf), m1cat, td=ts, n_blocks=ncf // ts, col_map=lambda c: c)
    kspec = _filter_spectrum(af[0], gcat, d=d, kb=kb, td=td)

    u4 = u.reshape(batch, h2, n1, 3 * d)
    zz4 = u4
    for o in range(HYENA_ORDER):
        a = _dft1(zz4, m1cat, td=ts, n_blocks=nts, col_map=lambda c: c)
        bp = _conv_mid(a, kspec, o, gcat, gtcat, kb=kb, td=td)
        zz4 = _idft1_gate(bp, minvcat, u4, zz4, p["skip"][o:o + 1], td=ts, gate_part=1 + o, zz_part=0)
    return zz4.reshape(t_rows, d)


def _trunk(x3, modall, row0, p):
    batch, seq, d = x3.shape
    t_rows = batch * seq
    x = x3.reshape(t_rows, d)
    n_kv = d // HEAD_DIM // GQA_GROUP
    depth = modall.shape[0]
    for i in range(depth):
        mod = modall[i, row0:row0 + batch].reshape(batch, 6, d)
        kind, j = i % 3, i // 3
        g_pre, g_post = p["norm_mix_pre"][i:i + 1], p["norm_mix_post"][i:i + 1]
        if kind == 0:
            x = _pool_layer(x, mod, g_pre, g_post, p["pool_w"][j], p["pool_scale"][j:j + 1], seq=seq)
        elif kind == 1:
            hp = {k: v[j] for k, v in p["hy"].items()}
            zz = _hyena_mixer(x, mod, g_pre, hp, batch=batch, seq=seq)
            x = _down(zz, hp["w_out"], x, mod, g_post, seq=seq, gate_idx=2)
        else:
            cos, sin = _rope_tables(seq)
            gains = jnp.stack([p["at_q_gain"][j], p["at_k_gain"][j]], axis=0)
            q, k, vt = _qkv(x, mod, g_pre, p["at_w_qkv"][j], gains, cos, sin, seq=seq, n_kv=n_kv)
            o = _flash(q, k, vt, batch=batch, seq=seq, n_kv=n_kv)
            x = _down(o, p["at_w_o"][j], x, mod, g_post, seq=seq, gate_idx=2)
        g = _up(x, mod, p["norm_ffn_pre"][i:i + 1], p["ffn_w_up"][i], None, p["ffn_conv_w"][i],
                p["ffn_conv_b"][i:i + 1], seq=seq, sh_idx=3, sc_idx=4, glu=True, out_dtype=BF16)
        x = _down(g, p["ffn_w_down"][i], x, mod, p["norm_ffn_post"][i:i + 1], seq=seq, gate_idx=5)
    return x.reshape(batch, seq, d)


def kernel(x_prompt, x_sample, c_prompt, c_sample, mod_w, mod_b, norm_mix_pre, norm_mix_post, norm_ffn_pre,
           norm_ffn_post, ffn_w_up, ffn_conv_w, ffn_conv_b, ffn_w_down, pool_w, pool_scale, hy_w_in, hy_b_in,
           hy_conv_w, hy_conv_b, hy_f_w1, hy_f_b1, hy_f_w2, hy_f_b2, hy_f_w3, hy_f_freq, hy_skip, hy_w_out,
           at_w_qkv, at_q_gain, at_k_gain, at_w_o):
    nb = c_prompt.shape[0]
    ns = c_sample.shape[0]
    assert nb + ns <= MOD_ROWS
    c8 = jnp.concatenate([c_prompt, c_sample, jnp.zeros((MOD_ROWS - nb - ns, c_prompt.shape[1]), F32)], axis=0)
    modall = _modulation(c8, mod_w, mod_b)
    row = lambda v: v[:, None, :]
    p = dict(
        norm_mix_pre=norm_mix_pre, norm_mix_post=norm_mix_post, norm_ffn_pre=norm_ffn_pre,
        norm_ffn_post=norm_ffn_post,
        ffn_w_up=ffn_w_up.astype(BF16), ffn_conv_w=ffn_conv_w, ffn_conv_b=ffn_conv_b,
        ffn_w_down=ffn_w_down.astype(BF16),
        pool_w=pool_w.astype(BF16), pool_scale=pool_scale,
        hy=dict(
            w_in=hy_w_in.astype(BF16), b_in=row(hy_b_in), conv_w=hy_conv_w, conv_b=row(hy_conv_b),
            f_w1=jnp.pad(hy_f_w1, ((0, 0), (0, HYENA_EMB_PAD - HYENA_EMB), (0, 0))), f_b1=row(hy_f_b1),
            f_w2=hy_f_w2, f_b2=row(hy_f_b2), f_w3=hy_f_w3, f_freq=row(hy_f_freq), skip=hy_skip,
            w_out=hy_w_out.astype(BF16)),
        at_w_qkv=at_w_qkv.astype(BF16), at_q_gain=at_q_gain, at_k_gain=at_k_gain, at_w_o=at_w_o.astype(BF16),
    )
    y_prompt = _trunk(x_prompt, modall, 0, p)
    y_sample = _trunk(x_sample, modall, nb, p)
    return (y_prompt, y_sample)
```

```python
import functools
import math

import jax
import jax.numpy as jnp
from jax import lax
from jax.experimental import pallas as pl
from jax.experimental.pallas import tpu as pltpu

EPS = 1e-6
HEAD_DIM = 128
GQA_GROUP = 4
GRID_W = 64
ROPE_THETA = 10000.0
POOL_WINDOWS = (2, 4, 8, 16)
HYENA_ORDER = 2
HYENA_EMB = 33
HYENA_BANDS = (HYENA_EMB - 1) // 2
HYENA_EMB_PAD = 64
HYENA_FAST_DECAY = 0.3
HYENA_SLOW_DECAY = 1.5
HYENA_TARGET = 1e-2
HYENA_MOD_SHIFT = 0.05
HYENA_MIN_DECAY = math.log(HYENA_TARGET) / HYENA_SLOW_DECAY
HYENA_MAX_DECAY = math.log(HYENA_TARGET) / HYENA_FAST_DECAY

HALO = 16
MOD_ROWS = 8
DFT_N2_MAX = 512
DFT_N1_MIN = 16
SUBLANES = 8
LANES = 128
UP_SUB = 512
FLASH_TK = 512
ONES_ROWS = 16
LOG2E = 1.4426950408889634
VMEM_LIMIT = 58 * 1024 * 1024

F32 = jnp.float32
BF16 = jnp.bfloat16


def _cparams(*sem):
    return pltpu.CompilerParams(dimension_semantics=sem, vmem_limit_bytes=VMEM_LIMIT)


def _tile(dim, pref):
    t = min(dim, pref)
    while dim % t:
        t //= 2
    return t


def _lane_tile(dim, cap):
    units = dim // LANES
    best = max(u for u in range(1, units + 1) if units % u == 0 and u * LANES <= max(cap, LANES))
    return best * LANES


def _split3(x):
    hi = x.astype(BF16)
    lo = (x - hi.astype(F32)).astype(BF16)
    return hi, lo


def _dot3(a, b):
    ah, al = _split3(a)
    bh, bl = _split3(b)
    d = functools.partial(jnp.dot, preferred_element_type=F32)
    return d(ah, bh) + (d(ah, bl) + d(al, bh))


def _dft_dot(m, x):
    return jnp.dot(m, x.astype(BF16), preferred_element_type=F32)


def _modnorm(xv, gain, scale1p, shift):
    ms = jnp.mean(xv * xv, axis=-1, keepdims=True)
    return xv * lax.rsqrt(ms + EPS) * gain * scale1p + shift


def _halo_maps(tm, t_rows):
    r = tm // HALO
    last = t_rows // HALO - 1
    prev = lambda i, *_: (jnp.maximum(i * r - 1, 0), 0)
    nxt = lambda i, *_: (jnp.minimum((i + 1) * r, last), 0)
    return prev, nxt


def _mod_kernel(c_ref, w_ref, b_ref, o_ref):
    c = c_ref[...]
    a = (c / (1.0 + jnp.exp(-c))).astype(BF16)
    o_ref[0] = jnp.dot(a, w_ref[0].astype(BF16), preferred_element_type=F32) + b_ref[0]


def _modulation(c8, mod_w, mod_b):
    depth, d, n = mod_w.shape
    tn = _tile(n, 1024)
    return pl.pallas_call(
        _mod_kernel,
        out_shape=jax.ShapeDtypeStruct((depth, MOD_ROWS, n), F32),
        grid=(depth, n // tn),
        in_specs=[
            pl.BlockSpec((MOD_ROWS, d), lambda l, j: (0, 0)),
            pl.BlockSpec((1, d, tn), lambda l, j: (l, 0, j)),
            pl.BlockSpec((1, 1, tn), lambda l, j: (l, 0, j)),
        ],
        out_specs=pl.BlockSpec((1, MOD_ROWS, tn), lambda l, j: (l, 0, j)),
        compiler_params=_cparams("arbitrary", "arbitrary"),
        name="modulation",
    )(c8, mod_w, mod_b.reshape(depth, 1, n))


def _fill_h(h_scr, x_ref, xp_ref, xn_ref, mod_ref, g_ref, sh_idx, sc_idx, tm):
    m = mod_ref[0]
    shift = m[sh_idx:sh_idx + 1]
    scale1p = 1.0 + m[sc_idx:sc_idx + 1]
    gain = g_ref[...]
    h_scr[0:HALO] = _modnorm(xp_ref[...], gain, scale1p, shift).astype(h_scr.dtype)
    h_scr[HALO:HALO + tm] = _modnorm(x_ref[...], gain, scale1p, shift).astype(h_scr.dtype)
    h_scr[HALO + tm:] = _modnorm(xn_ref[...], gain, scale1p, shift).astype(h_scr.dtype)


def _edge_keep(tps):
    i = pl.program_id(0) % tps
    return (i != 0).astype(F32), (i != tps - 1).astype(F32)


def _conv3_rows(a, keep_first, keep_last, cw, cb, tm):
    rows = tm + 2 * HALO
    a = jnp.concatenate([a[0:HALO] * keep_first, a[HALO:HALO + tm], a[HALO + tm:] * keep_last], axis=0)
    prev = pltpu.roll(a, 1, 0)
    nxt = pltpu.roll(a, rows - 1, 0)
    c = prev * cw[0:1] + a * cw[1:2] + nxt * cw[2:3] + cb
    return c[HALO:HALO + tm]


def _gelu_tanh(x):
    return 0.5 * x * (1.0 + jnp.tanh(0.7978845608028654 * (x + 0.044715 * (x * x * x))))


def _up_kernel(*refs, sh_idx, sc_idx, tm, tps, glu, has_bias):
    x_ref, xp_ref, xn_ref, mod_ref, g_ref, w_ref = refs[:6]
    k = 6
    wb_ref = b_ref = None
    if glu:
        wb_ref = refs[k]
        k += 1
    if has_bias:
        b_ref = refs[k]
        k += 1
    cw_ref, cb_ref, o_ref, h_scr = refs[k:k + 4]

    @pl.when(pl.program_id(1) == 0)
    def _():
        _fill_h(h_scr, x_ref, xp_ref, xn_ref, mod_ref, g_ref, sh_idx, sc_idx, tm)

    keep_first, keep_last = _edge_keep(tps)
    for s in range(o_ref.shape[1] // UP_SUB):
        sl = slice(s * UP_SUB, (s + 1) * UP_SUB)
        a = jnp.dot(h_scr[...], w_ref[:, sl], preferred_element_type=F32)
        if has_bias:
            a = a + b_ref[:, sl]
        c = _conv3_rows(a, keep_first, keep_last, cw_ref[:, sl], cb_ref[:, sl], tm)
        if glu:
            b = jnp.dot(h_scr[HALO:HALO + tm], wb_ref[:, sl], preferred_element_type=F32)
            c = _gelu_tanh(c) * b
        o_ref[:, sl] = c.astype(o_ref.dtype)


def _up(x, mod, gain, w, bias, cw, cb, *, seq, sh_idx, sc_idx, glu, out_dtype):
    t_rows, d = x.shape
    n_out = cw.shape[-1]
    tm = _tile(seq, 1024)
    tn = _tile(n_out, UP_SUB if glu else 2 * UP_SUB)
    assert tn % UP_SUB == 0
    tps = seq // tm
    nj = n_out // tn
    prev, nxt = _halo_maps(tm, t_rows)
    in_specs = [
        pl.BlockSpec((tm, d), lambda i, j: (i, 0)),
        pl.BlockSpec((HALO, d), prev),
        pl.BlockSpec((HALO, d), nxt),
        pl.BlockSpec((1, 6, d), lambda i, j: (i // tps, 0, 0)),
        pl.BlockSpec((1, d), lambda i, j: (0, 0)),
        pl.BlockSpec((d, tn), lambda i, j: (0, j)),
    ]
    args = [x, x, x, mod, gain, w]
    if glu:
        in_specs.append(pl.BlockSpec((d, tn), lambda i, j: (0, j + nj)))
        args.append(w)
    if bias is not None:
        in_specs.append(pl.BlockSpec((1, tn), lambda i, j: (0, j)))
        args.append(bias)
    in_specs += [pl.BlockSpec((3, tn), lambda i, j: (0, j)), pl.BlockSpec((1, tn), lambda i, j: (0, j))]
    args += [cw, cb]
    return pl.pallas_call(
        functools.partial(_up_kernel, sh_idx=sh_idx, sc_idx=sc_idx, tm=tm, tps=tps, glu=glu,
                          has_bias=bias is not None),
        out_shape=jax.ShapeDtypeStruct((t_rows, n_out), out_dtype),
        grid=(t_rows // tm, nj),
        in_specs=in_specs,
        out_specs=pl.BlockSpec((tm, tn), lambda i, j: (i, j)),
        scratch_shapes=[pltpu.VMEM((tm + 2 * HALO, d), BF16)],
        compiler_params=_cparams("arbitrary", "arbitrary"),
        name="up_glu" if glu else "up_conv",
    )(*args)


def _down_kernel(g_ref, w_ref, x_ref, mod_ref, gain_ref, o_ref, acc_ref, *, gate_idx, nk):
    k = pl.program_id(1)
    part = jnp.dot(g_ref[...].astype(w_ref.dtype), w_ref[...], preferred_element_type=F32)

    @pl.when(k == 0)
    def _():
        acc_ref[...] = part

    @pl.when(k > 0)
    def _():
        acc_ref[...] += part

    @pl.when(k == nk - 1)
    def _():
        f = acc_ref[...]
        ms = jnp.mean(f * f, axis=-1, keepdims=True)
        y = f * lax.rsqrt(ms + EPS) * gain_ref[...]
        gate = mod_ref[0][gate_idx:gate_idx + 1]
        o_ref[...] = x_ref[...] + gate * y


def _down(g, w, x, mod, gain, *, seq, gate_idx):
    t_rows, kdim = g.shape
    d = w.shape[1]
    tm = _tile(seq, 512)
    tk = _lane_tile(kdim, 2816)
    tps = seq // tm
    nk = kdim // tk
    return pl.pallas_call(
        functools.partial(_down_kernel, gate_idx=gate_idx, nk=nk),
        out_shape=jax.ShapeDtypeStruct((t_rows, d), F32),
        grid=(t_rows // tm, nk),
        in_specs=[
            pl.BlockSpec((tm, tk), lambda i, k: (i, k)),
            pl.BlockSpec((tk, d), lambda i, k: (k, 0)),
            pl.BlockSpec((tm, d), lambda i, k: (i, 0)),
            pl.BlockSpec((1, 6, d), lambda i, k: (i // tps, 0, 0)),
            pl.BlockSpec((1, d), lambda i, k: (0, 0)),
        ],
        out_specs=pl.BlockSpec((tm, d), lambda i, k: (i, 0)),
        scratch_shapes=[pltpu.VMEM((tm, d), F32)],
        compiler_params=_cparams("arbitrary", "arbitrary"),
        name="down",
    )(g, w, x, mod, gain)


def _pool_kernel(x_ref, xp_ref, xn_ref, mod_ref, gpre_ref, gpost_ref, pw_ref, ps_ref, o_ref, h_scr,
                 *, tm, tps, seq):
    _fill_h(h_scr, x_ref, xp_ref, xn_ref, mod_ref, gpre_ref, 0, 1, tm)
    keep_first, keep_last = _edge_keep(tps)
    rows = tm + 2 * HALO
    d = x_ref.shape[1]
    cg = d // len(POOL_WINDOWS)
    pos = (pl.program_id(0) % tps) * tm + lax.broadcasted_iota(jnp.int32, (tm, 1), 0)
    ys = []
    ssq = jnp.zeros((tm, 1), F32)
    for g, win in enumerate(POOL_WINDOWS):
        half = win // 2
        sl = slice(g * cg, (g + 1) * cg)
        hg = jnp.concatenate([h_scr[0:HALO, sl] * keep_first, h_scr[HALO:HALO + tm, sl],
                              h_scr[HALO + tm:, sl] * keep_last], axis=0)
        p = hg
        s = 1
        while s < win:
            p = p + pltpu.roll(p, s, 0)
            s *= 2
        if half > 1:
            p = pltpu.roll(p, rows - (half - 1), 0)
        lo = jnp.maximum(pos - half, 0)
        hi = jnp.minimum(pos + (half - 1), seq - 1)
        cnt = (hi - lo + 1).astype(F32)
        pooled = p[HALO:HALO + tm] / cnt - hg[HALO:HALO + tm]
        y = jnp.dot(pooled.astype(BF16), pw_ref[g], preferred_element_type=F32) * ps_ref[:, sl]
        ssq = ssq + jnp.sum(y * y, axis=-1, keepdims=True)
        ys.append(y)
    inv = lax.rsqrt(ssq / d + EPS)
    gate = mod_ref[0][2:3]
    for g in range(len(POOL_WINDOWS)):
        sl = slice(g * cg, (g + 1) * cg)
        o_ref[:, sl] = x_ref[:, sl] + gate[:, sl] * (ys[g] * inv * gpost_ref[:, sl])


def _pool_layer(x, mod, gpre, gpost, pw, ps, *, seq):
    t_rows, d = x.shape
    tm = _tile(seq, 256)
    tps = seq // tm
    prev, nxt = _halo_maps(tm, t_rows)
    return pl.pallas_call(
        functools.partial(_pool_kernel, tm=tm, tps=tps, seq=seq),
        out_shape=jax.ShapeDtypeStruct((t_rows, d), F32),
        grid=(t_rows // tm,),
        in_specs=[
            pl.BlockSpec((tm, d), lambda i: (i, 0)),
            pl.BlockSpec((HALO, d), prev),
            pl.BlockSpec((HALO, d), nxt),
            pl.BlockSpec((1, 6, d), lambda i: (i // tps, 0, 0)),
            pl.BlockSpec((1, d), lambda i: (0, 0)),
            pl.BlockSpec((1, d), lambda i: (0, 0)),
            pl.BlockSpec(pw.shape, lambda i: (0, 0, 0)),
            pl.BlockSpec((1, d), lambda i: (0, 0)),
        ],
        out_specs=pl.BlockSpec((tm, d), lambda i: (i, 0)),
        scratch_shapes=[pltpu.VMEM((tm + 2 * HALO, d), F32)],
        compiler_params=_cparams("arbitrary"),
        name="pool_layer",
    )(x, x, x, mod, gpre, gpost, pw, ps)


def _rope_tables(seq):
    pos = jnp.arange(seq, dtype=jnp.int32)
    row = (pos // GRID_W).astype(F32)[:, None]
    col = (pos % GRID_W).astype(F32)[:, None]
    axis_dim = HEAD_DIM // 2
    inv = ROPE_THETA ** (-jnp.arange(0, axis_dim, 2, dtype=F32) / axis_dim)
    ar, ac = row * inv, col * inv
    cos = jnp.concatenate([jnp.cos(ar), jnp.cos(ar), jnp.cos(ac), jnp.cos(ac)], axis=-1)
    sin = jnp.concatenate([-jnp.sin(ar), jnp.sin(ar), -jnp.sin(ac), jnp.sin(ac)], axis=-1)
    return cos, sin


def _qkv_kernel(x_ref, mod_ref, g_ref, w_ref, hg_ref, cos_ref, sin_ref, q_ref, k_ref, vt_ref, h_scr, *, n_q_tiles):
    j = pl.program_id(1)

    @pl.when(j == 0)
    def _():
        m = mod_ref[0]
        h_scr[...] = _modnorm(x_ref[...], g_ref[...], 1.0 + m[1:2], m[0:1]).astype(BF16)

    a = jnp.dot(h_scr[...], w_ref[...], preferred_element_type=F32)

    def norm_rope(gain, out_scale, o_ref):
        cos = cos_ref[...]
        sin = sin_ref[...]
        quarter = HEAD_DIM // 4
        lane = lax.broadcasted_iota(jnp.int32, (1, HEAD_DIM), 1)
        low = (lane % (2 * quarter)) < quarter
        for h in range(a.shape[1] // HEAD_DIM):
            sl = slice(h * HEAD_DIM, (h + 1) * HEAD_DIM)
            v = a[:, sl]
            ms = jnp.mean(v * v, axis=-1, keepdims=True)
            v = v * lax.rsqrt(ms + EPS) * gain
            swapped = jnp.where(low, pltpu.roll(v, HEAD_DIM - quarter, 1), pltpu.roll(v, quarter, 1))
            o_ref[:, sl] = ((v * cos + swapped * sin) * out_scale).astype(o_ref.dtype)

    @pl.when(j < n_q_tiles)
    def _():
        norm_rope(hg_ref[0:1], HEAD_DIM ** -0.5 * LOG2E, q_ref)

    @pl.when(j == n_q_tiles)
    def _():
        norm_rope(hg_ref[1:2], 1.0, k_ref)

    @pl.when(j == n_q_tiles + 1)
    def _():
        for c in range(vt_ref.shape[0]):
            vt_ref[c] = a[c * FLASH_TK:(c + 1) * FLASH_TK].T.astype(vt_ref.dtype)


def _qkv(x, mod, gain, w, head_gains, cos, sin, *, seq, n_kv):
    t_rows, d = x.shape
    n_out = w.shape[1]
    tm = _tile(seq, 512)
    tn = HEAD_DIM * n_kv
    tps = seq // tm
    n_q_tiles = d // tn
    assert tm % FLASH_TK == 0 and n_out == d + 2 * tn
    return pl.pallas_call(
        functools.partial(_qkv_kernel, n_q_tiles=n_q_tiles),
        out_shape=(jax.ShapeDtypeStruct((t_rows, d), BF16), jax.ShapeDtypeStruct((t_rows, tn), BF16),
                   jax.ShapeDtypeStruct((t_rows // FLASH_TK, tn, FLASH_TK), BF16)),
        grid=(t_rows // tm, n_out // tn),
        in_specs=[
            pl.BlockSpec((tm, d), lambda i, j: (i, 0)),
            pl.BlockSpec((1, 6, d), lambda i, j: (i // tps, 0, 0)),
            pl.BlockSpec((1, d), lambda i, j: (0, 0)),
            pl.BlockSpec((d, tn), lambda i, j: (0, j)),
            pl.BlockSpec((2, HEAD_DIM), lambda i, j: (0, 0)),
            pl.BlockSpec((tm, HEAD_DIM), lambda i, j: (i % tps, 0)),
            pl.BlockSpec((tm, HEAD_DIM), lambda i, j: (i % tps, 0)),
        ],
        out_specs=(
            pl.BlockSpec((tm, tn), lambda i, j: (i, jnp.minimum(j, n_q_tiles - 1))),
            pl.BlockSpec((tm, tn), lambda i, j: (i, 0)),
            pl.BlockSpec((tm // FLASH_TK, tn, FLASH_TK), lambda i, j: (i, 0, 0)),
        ),
        scratch_shapes=[pltpu.VMEM((tm, d), BF16)],
        compiler_params=_cparams("arbitrary", "arbitrary"),
        name="qkv_rope",
    )(x, mod, gain, w, head_gains, cos, sin)


def _flash_kernel(q_ref, k_ref, vt_ref, o_ref, acc_ref, s_ref, *, tq, nk):
    q = q_ref[...]
    qs = jnp.concatenate([q[:, h * HEAD_DIM:(h + 1) * HEAD_DIM] for h in range(GQA_GROUP)], axis=0)
    ones = jnp.ones((ONES_ROWS, FLASH_TK), BF16)
    acc_ref[...] = jnp.zeros_like(acc_ref)

    def scores(kk, slot):
        start = pl.multiple_of(kk * FLASH_TK, FLASH_TK)
        s_ref[slot] = lax.dot_general(k_ref[pl.ds(start, FLASH_TK), :], qs, (((1,), (1,)), ((), ())),
                                      preferred_element_type=F32)

    def accumulate(kk, slot, m):
        st = s_ref[slot]
        m_new = jnp.maximum(m, jnp.max(st, axis=0, keepdims=True))
        alpha = jnp.exp2(m - m_new)
        pt = jnp.exp2(st - m_new).astype(BF16)
        va = jnp.concatenate([vt_ref[kk], ones], axis=0)
        acc_ref[...] = alpha * acc_ref[...] + jnp.dot(va, pt, preferred_element_type=F32)
        return m_new

    scores(0, 0)

    def pair(i, m):
        k0 = 2 * i
        scores(k0 + 1, 1)
        m = accumulate(k0, 0, m)
        scores(jnp.minimum(k0 + 2, nk - 1), 0)
        return accumulate(k0 + 1, 1, m)

    lax.fori_loop(0, nk // 2, pair, jnp.full((1, GQA_GROUP * tq), -jnp.inf, F32))
    acc = acc_ref[...]
    ot = acc[:HEAD_DIM] / acc[HEAD_DIM:HEAD_DIM + 1]
    for h in range(GQA_GROUP):
        o_ref[:, h * HEAD_DIM:(h + 1) * HEAD_DIM] = ot[:, h * tq:(h + 1) * tq].T.astype(o_ref.dtype)


def _flash(q, k, vt, *, batch, seq, n_kv):
    t_rows, d = q.shape
    tq = _tile(seq, 512)
    nqb = seq // tq
    nk = seq // FLASH_TK
    assert nk % 2 == 0
    gw = GQA_GROUP * HEAD_DIM
    return pl.pallas_call(
        functools.partial(_flash_kernel, tq=tq, nk=nk),
        out_shape=jax.ShapeDtypeStruct((t_rows, d), BF16),
        grid=(batch, n_kv, nqb),
        in_specs=[
            pl.BlockSpec((tq, gw), lambda b, kv, qi: (b * nqb + qi, kv)),
            pl.BlockSpec((seq, HEAD_DIM), lambda b, kv, qi: (b, kv)),
            pl.BlockSpec((nk, HEAD_DIM, FLASH_TK), lambda b, kv, qi: (b, kv, 0)),
        ],
        out_specs=pl.BlockSpec((tq, gw), lambda b, kv, qi: (b * nqb + qi, kv)),
        scratch_shapes=[pltpu.VMEM((HEAD_DIM + ONES_ROWS, GQA_GROUP * tq), F32),
                        pltpu.VMEM((2, FLASH_TK, GQA_GROUP * tq), F32)],
        compiler_params=_cparams("arbitrary", "arbitrary", "arbitrary"),
        name="flash_attention",
    )(q, k, vt)


def _dft_sizes(seq):
    n = 2 * seq
    n2 = DFT_N2_MAX
    while n // n2 < DFT_N1_MIN:
        n2 //= 2
    return n, n // n2, n2


def _dft_tables(seq):
    n, n1, n2 = _dft_sizes(seq)
    h2 = n2 // 2
    k2 = jnp.arange(h2, dtype=jnp.int32)
    m2 = jnp.arange(h2, dtype=jnp.int32)
    idx = (m2[None, :] * (2 * k2[:, None] + 1)) % (2 * n2)
    th = idx.astype(F32) * (2.0 * math.pi / (2 * n2))
    m1 = jnp.concatenate([jnp.cos(th), -jnp.sin(th)], axis=0)
    a1 = jnp.arange(n1, dtype=jnp.int32)
    k1 = jnp.arange(n1, dtype=jnp.int32)
    freq = 2 * n2 * k1[None, :, None] + 2 * k2[:, None, None] + 1
    idx = (a1[None, None, :] * freq) % (2 * n)
    ph = idx.astype(F32) * (2.0 * math.pi / (2 * n))
    c, s = jnp.cos(ph), jnp.sin(ph)
    gb = jnp.concatenate([jnp.concatenate([c, s], axis=2), jnp.concatenate([-s, c], axis=2)], axis=1)
    return m1, gb


def _filt_kernel(z_ref, w1_ref, b1_ref, w2_ref, b2_ref, fr_ref, w3_ref, dl_ref, o_ref):
    z = z_ref[...]
    fr = fr_ref[...]
    a = jnp.sin(fr * (_dot3(z, w1_ref[...]) + b1_ref[...]))
    a = jnp.sin(fr * (_dot3(a, w2_ref[...]) + b2_ref[...]))
    decay = jnp.exp(-z[:, 0:1] * dl_ref[...]) + HYENA_MOD_SHIFT
    d = decay.shape[1]
    row = pl.program_id(0) * z.shape[0] + lax.broadcasted_iota(jnp.int32, (z.shape[0], 1), 0)
    for c in range(w3_ref.shape[1] // d):
        f = _dot3(a, w3_ref[:, c * d:(c + 1) * d]) * decay
        o_ref[:, c * d:(c + 1) * d] = jnp.where(row == 0, 0.0, f) if c % 2 == 1 else f


def _filters(z, w1, b1, w2, b2, fr, w3, deltas):
    seq = z.shape[0]
    d = deltas.shape[1]
    tl = _tile(seq, 128)
    hid = w2.shape[0]
    const = lambda i: (0, 0)
    return pl.pallas_call(
        _filt_kernel,
        out_shape=jax.ShapeDtypeStruct((seq, w3.shape[1]), F32),
        grid=(seq // tl,),
        in_specs=[
            pl.BlockSpec((tl, HYENA_EMB_PAD), lambda i: (i, 0)),
            pl.BlockSpec((HYENA_EMB_PAD, hid), const),
            pl.BlockSpec((1, hid), const),
            pl.BlockSpec((hid, hid), const),
            pl.BlockSpec((1, hid), const),
            pl.BlockSpec((1, hid), const),
            pl.BlockSpec(w3.shape, const),
            pl.BlockSpec((1, d), const),
        ],
        out_specs=pl.BlockSpec((tl, w3.shape[1]), lambda i: (i, 0)),
        compiler_params=_cparams("arbitrary"),
        name="hyena_filters",
    )(z, w1, b1, w2, b2, fr, w3, deltas)


def _dft1_kernel(x_ref, m_ref, o_ref, y_scr):
    xt = pltpu.einshape("hjd->jhd", x_ref[0])
    m = m_ref[...]
    for j in range(SUBLANES):
        y_scr[j] = _dft_dot(m, xt[j])
    yt = pltpu.einshape("jnd->njd", y_scr[...])
    h2 = yt.shape[0] // 2
    o_ref[0, 0] = yt[:h2]
    o_ref[0, 1] = yt[h2:]


def _dft1(x4, m1b, *, td, n_blocks, col_map):
    b, h2, n1, _ = x4.shape
    n2 = m1b.shape[0]
    return pl.pallas_call(
        _dft1_kernel,
        out_shape=jax.ShapeDtypeStruct((b, 2, h2, n1, n_blocks * td), F32),
        grid=(b, n1 // SUBLANES, n_blocks),
        in_specs=[
            pl.BlockSpec((1, h2, SUBLANES, td), lambda bi, g, c: (bi, 0, g, col_map(c))),
            pl.BlockSpec(m1b.shape, lambda bi, g, c: (0, 0)),
        ],
        out_specs=pl.BlockSpec((1, 2, h2, SUBLANES, td), lambda bi, g, c: (bi, 0, 0, g, c)),
        scratch_shapes=[pltpu.VMEM((SUBLANES, n2, td), F32)],
        compiler_params=_cparams("arbitrary", "arbitrary", "arbitrary"),
        name="dft_stage1",
    )(x4, m1b)


def _fspec_kernel(af_ref, ab_ref, g_ref, o_ref, *, kb, n1):
    for kk in range(kb):
        xf = _dft_dot(g_ref[kk], jnp.concatenate([af_ref[0, kk], af_ref[1, kk]], axis=0))
        xb = _dft_dot(g_ref[kk], jnp.concatenate([ab_ref[0, kk], ab_ref[1, kk]], axis=0))
        o_ref[0, 0, kk] = xf[:n1] + xb[:n1]
        o_ref[0, 1, kk] = xf[n1:] - xb[n1:]


def _filter_spectrum(af, gbb, *, d, kb, td):
    _, h2, n1, w = af.shape
    ndt = d // td
    norder = w // (2 * d)
    return pl.pallas_call(
        functools.partial(_fspec_kernel, kb=kb, n1=n1),
        out_shape=jax.ShapeDtypeStruct((norder, 2, h2, n1, d), F32),
        grid=(norder, h2 // kb, ndt),
        in_specs=[
            pl.BlockSpec((2, kb, n1, td), lambda o, k, t: (0, k, 0, (2 * o) * ndt + t)),
            pl.BlockSpec((2, kb, n1, td), lambda o, k, t: (0, k, 0, (2 * o + 1) * ndt + t)),
            pl.BlockSpec((kb, 2 * n1, 2 * n1), lambda o, k, t: (k, 0, 0)),
        ],
        out_specs=pl.BlockSpec((1, 2, kb, n1, td), lambda o, k, t: (o, 0, k, 0, t)),
        compiler_params=_cparams("arbitrary", "arbitrary", "arbitrary"),
        name="hyena_filter_spectrum",
    )(af, af, gbb)


def _cmid_kernel(a_ref, ks_ref, g_ref, gt_ref, o_ref, *, kb, n1):
    for kk in range(kb):
        x = _dft_dot(g_ref[kk], jnp.concatenate([a_ref[0, 0, kk], a_ref[0, 1, kk]], axis=0))
        xr, xi = x[:n1], x[n1:]
        kr, ki = ks_ref[0, 0, kk], ks_ref[0, 1, kk]
        y = jnp.concatenate([xr * kr - xi * ki, xr * ki + xi * kr], axis=0)
        bp = _dft_dot(gt_ref[kk], y)
        o_ref[0, 0, kk] = bp[:n1]
        o_ref[0, 1, kk] = bp[n1:]


def _conv_mid(a, kspec, order, gbb, gtb, *, kb, td):
    b, _, h2, n1, d = a.shape
    return pl.pallas_call(
        functools.partial(_cmid_kernel, kb=kb, n1=n1),
        out_shape=jax.ShapeDtypeStruct(a.shape, F32),
        grid=(h2 // kb, d // td, b),
        in_specs=[
            pl.BlockSpec((1, 2, kb, n1, td), lambda k, t, bi: (bi, 0, k, 0, t)),
            pl.BlockSpec((1, 2, kb, n1, td), lambda k, t, bi: (order, 0, k, 0, t)),
            pl.BlockSpec((kb, 2 * n1, 2 * n1), lambda k, t, bi: (k, 0, 0)),
            pl.BlockSpec((kb, 2 * n1, 2 * n1), lambda k, t, bi: (k, 0, 0)),
        ],
        out_specs=pl.BlockSpec((1, 2, kb, n1, td), lambda k, t, bi: (bi, 0, k, 0, t)),
        compiler_params=_cparams("arbitrary", "arbitrary", "arbitrary"),
        name="hyena_conv_mid",
    )(a, kspec, gbb, gtb)


def _idft1_kernel(bp_ref, m_ref, gate_ref, zz_ref, skip_ref, o_ref, y_scr):
    bp = jnp.concatenate([bp_ref[0, 0], bp_ref[0, 1]], axis=0)
    bt = pltpu.einshape("njd->jnd", bp)
    m = m_ref[...]
    for j in range(SUBLANES):
        y_scr[j] = _dft_dot(m, bt[j])
    y = pltpu.einshape("jhd->hjd", y_scr[...])
    o_ref[0] = (gate_ref[0] * (y + zz_ref[0] * skip_ref[...])).astype(o_ref.dtype)


def _idft1_gate(bp, minvb, u4, zz4, skip, *, td, gate_part, zz_part):
    b, _, h2, n1, d = bp.shape
    nt = d // td
    return pl.pallas_call(
        _idft1_kernel,
        out_shape=jax.ShapeDtypeStruct((b, h2, n1, d), F32),
        grid=(b, n1 // SUBLANES, nt),
        in_specs=[
            pl.BlockSpec((1, 2, h2, SUBLANES, td), lambda bi, g, c: (bi, 0, 0, g, c)),
            pl.BlockSpec(minvb.shape, lambda bi, g, c: (0, 0)),
            pl.BlockSpec((1, h2, SUBLANES, td), lambda bi, g, c: (bi, 0, g, gate_part * nt + c)),
            pl.BlockSpec((1, h2, SUBLANES, td), lambda bi, g, c: (bi, 0, g, zz_part * nt + c)),
            pl.BlockSpec((1, td), lambda bi, g, c: (0, c)),
        ],
        out_specs=pl.BlockSpec((1, h2, SUBLANES, td), lambda bi, g, c: (bi, 0, g, c)),
        scratch_shapes=[pltpu.VMEM((SUBLANES, h2, td), F32)],
        compiler_params=_cparams("arbitrary", "arbitrary", "arbitrary"),
        name="idft_stage1_gate",
    )(bp, minvb, u4, zz4, skip)


def _hyena_features(seq, d):
    t = jnp.linspace(0.0, 1.0, seq, dtype=F32)[:, None]
    w = 2.0 * math.pi * jnp.arange(seq, dtype=F32)[:, None] / seq
    f = jnp.linspace(1e-4, HYENA_BANDS - 1, HYENA_BANDS, dtype=F32)[None, :]
    z = jnp.concatenate([t, jnp.cos(f * w), -jnp.sin(f * w)], axis=-1)
    z = jnp.pad(z, ((0, 0), (0, HYENA_EMB_PAD - HYENA_EMB)))
    deltas = jnp.abs(jnp.linspace(HYENA_MIN_DECAY, HYENA_MAX_DECAY, d, dtype=F32))[None, :]
    return z, deltas


def _hyena_mixer(x, mod, gain, p, *, batch, seq):
    t_rows, d = x.shape
    n, n1, n2 = _dft_sizes(seq)
    h2 = n2 // 2
    u = _up(x, mod, gain, p["w_in"], p["b_in"], p["conv_w"], p["conv_b"], seq=seq, sh_idx=0, sc_idx=1,
            glu=False, out_dtype=F32)

    z, deltas = _hyena_features(seq, d)
    filt = _filters(z, p["f_w1"], p["f_b1"], p["f_w2"], p["f_b2"], p["f_freq"], p["f_w3"], deltas)

    m1, gb = _dft_tables(seq)
    m1b = m1.astype(BF16)
    minvb = (m1.T * (2.0 / n)).astype(BF16)
    gbb = gb.astype(BF16)
    gtb = jnp.swapaxes(gb, 1, 2).astype(BF16)
    kb = max(1, 256 // n1)
    td = _tile(d, 1024)
    ts = _tile(d, 256)
    nts = d // ts

    ncf = filt.shape[1]
    af = _dft1(filt.reshape(1, h2, n1, ncf), m1b, td=ts, n_blocks=ncf // ts, col_map=lambda c: c)
    kspec = _filter_spectrum(af[0], gbb, d=d, kb=kb, td=td)

    u4 = u.reshape(batch, h2, n1, 3 * d)
    zz4 = u4
    for o in range(HYENA_ORDER):
        a = _dft1(zz4, m1b, td=ts, n_blocks=nts, col_map=lambda c: c)
        bp = _conv_mid(a, kspec, o, gbb, gtb, kb=kb, td=td)
        zz4 = _idft1_gate(bp, minvb, u4, zz4, p["skip"][o:o + 1], td=ts, gate_part=1 + o, zz_part=0)
    return zz4.reshape(t_rows, d)


def _trunk(x3, modall, row0, p):
    batch, seq, d = x3.shape
    t_rows = batch * seq
    x = x3.reshape(t_rows, d)
    n_kv = d // HEAD_DIM // GQA_GROUP
    depth = modall.shape[0]
    for i in range(depth):
        mod = modall[i, row0:row0 + batch].reshape(batch, 6, d)
        kind, j = i % 3, i // 3
        g_pre, g_post = p["norm_mix_pre"][i:i + 1], p["norm_mix_post"][i:i + 1]
        if kind == 0:
            x = _pool_layer(x, mod, g_pre, g_post, p["pool_w"][j], p["pool_scale"][j:j + 1], seq=seq)
        elif kind == 1:
            hp = {k: v[j] for k, v in p["hy"].items()}
            zz = _hyena_mixer(x, mod, g_pre, hp, batch=batch, seq=seq)
            x = _down(zz, hp["w_out"], x, mod, g_post, seq=seq, gate_idx=2)
        else:
            cos, sin = _rope_tables(seq)
            gains = jnp.stack([p["at_q_gain"][j], p["at_k_gain"][j]], axis=0)
            q, k, vt = _qkv(x, mod, g_pre, p["at_w_qkv"][j], gains, cos, sin, seq=seq, n_kv=n_kv)
            o = _flash(q, k, vt, batch=batch, seq=seq, n_kv=n_kv)
            x = _down(o, p["at_w_o"][j], x, mod, g_post, seq=seq, gate_idx=2)
        g = _up(x, mod, p["norm_ffn_pre"][i:i + 1], p["ffn_w_up"][i], None, p["ffn_conv_w"][i],
                p["ffn_conv_b"][i:i + 1], seq=seq, sh_idx=3, sc_idx=4, glu=True, out_dtype=BF16)
        x = _down(g, p["ffn_w_down"][i], x, mod, p["norm_ffn_post"][i:i + 1], seq=seq, gate_idx=5)
    return x.reshape(batch, seq, d)


def kernel(x_prompt, x_sample, c_prompt, c_sample, mod_w, mod_b, norm_mix_pre, norm_mix_post, norm_ffn_pre,
           norm_ffn_post, ffn_w_up, ffn_conv_w, ffn_conv_b, ffn_w_down, pool_w, pool_scale, hy_w_in, hy_b_in,
           hy_conv_w, hy_conv_b, hy_f_w1, hy_f_b1, hy_f_w2, hy_f_b2, hy_f_w3, hy_f_freq, hy_skip, hy_w_out,
           at_w_qkv, at_q_gain, at_k_gain, at_w_o):
    nb = c_prompt.shape[0]
    ns = c_sample.shape[0]
    assert nb + ns <= MOD_ROWS
    c8 = jnp.concatenate([c_prompt, c_sample, jnp.zeros((MOD_ROWS - nb - ns, c_prompt.shape[1]), F32)], axis=0)
    modall = _modulation(c8, mod_w, mod_b)
    row = lambda v: v[:, None, :]
    p = dict(
        norm_mix_pre=norm_mix_pre, norm_mix_post=norm_mix_post, norm_ffn_pre=norm_ffn_pre,
        norm_ffn_post=norm_ffn_post,
        ffn_w_up=ffn_w_up.astype(BF16), ffn_conv_w=ffn_conv_w, ffn_conv_b=ffn_conv_b,
        ffn_w_down=ffn_w_down.astype(BF16),
        pool_w=pool_w.astype(BF16), pool_scale=pool_scale,
        hy=dict(
            w_in=hy_w_in.astype(BF16), b_in=row(hy_b_in), conv_w=hy_conv_w, conv_b=row(hy_conv_b),
            f_w1=jnp.pad(hy_f_w1, ((0, 0), (0, HYENA_EMB_PAD - HYENA_EMB), (0, 0))), f_b1=row(hy_f_b1),
            f_w2=hy_f_w2, f_b2=row(hy_f_b2), f_w3=hy_f_w3, f_freq=row(hy_f_freq), skip=hy_skip,
            w_out=hy_w_out.astype(BF16)),
        at_w_qkv=at_w_qkv.astype(BF16), at_q_gain=at_q_gain, at_k_gain=at_k_gain, at_w_o=at_w_o.astype(BF16),
    )
    y_prompt = _trunk(x_prompt, modall, 0, p)
    y_sample = _trunk(x_sample, modall, nb, p)
    return (y_prompt, y_sample)
```

```python
import functools
import math

import jax
import jax.numpy as jnp
from jax import lax
from jax.experimental import pallas as pl
from jax.experimental.pallas import tpu as pltpu

EPS = 1e-6
HEAD_DIM = 128
GQA_GROUP = 4
GRID_W = 64
ROPE_THETA = 10000.0
POOL_WINDOWS = (2, 4, 8, 16)
HYENA_ORDER = 2
HYENA_EMB = 33
HYENA_BANDS = (HYENA_EMB - 1) // 2
HYENA_EMB_PAD = 64
HYENA_FAST_DECAY = 0.3
HYENA_SLOW_DECAY = 1.5
HYENA_TARGET = 1e-2
HYENA_MOD_SHIFT = 0.05
HYENA_MIN_DECAY = math.log(HYENA_TARGET) / HYENA_SLOW_DECAY
HYENA_MAX_DECAY = math.log(HYENA_TARGET) / HYENA_FAST_DECAY

HALO = 16
MOD_ROWS = 8
DFT_N2_MAX = 512
DFT_N1_MIN = 16
SUBLANES = 8
LANES = 128
UP_SUB = 512
FLASH_TK = 512
ONES_ROWS = 16
LOG2E = 1.4426950408889634
VMEM_LIMIT = 58 * 1024 * 1024

F32 = jnp.float32
BF16 = jnp.bfloat16


def _cparams(*sem):
    return pltpu.CompilerParams(dimension_semantics=sem, vmem_limit_bytes=VMEM_LIMIT)


def _tile(dim, pref):
    t = min(dim, pref)
    while dim % t:
        t //= 2
    return t


def _lane_tile(dim, cap):
    units = dim // LANES
    best = max(u for u in range(1, units + 1) if units % u == 0 and u * LANES <= max(cap, LANES))
    return best * LANES


def _split3(x):
    hi = x.astype(BF16)
    lo = (x - hi.astype(F32)).astype(BF16)
    return hi, lo


def _dot3(a, b):
    ah, al = _split3(a)
    bh, bl = _split3(b)
    d = functools.partial(jnp.dot, preferred_element_type=F32)
    return d(ah, bh) + (d(ah, bl) + d(al, bh))


def _dft_dot(m, x):
    return jnp.dot(m, x.astype(BF16), preferred_element_type=F32)


def _modnorm(xv, gain, scale1p, shift):
    ms = jnp.mean(xv * xv, axis=-1, keepdims=True)
    return xv * lax.rsqrt(ms + EPS) * gain * scale1p + shift


def _halo_maps(tm, t_rows):
    r = tm // HALO
    last = t_rows // HALO - 1
    prev = lambda i, *_: (jnp.maximum(i * r - 1, 0), 0)
    nxt = lambda i, *_: (jnp.minimum((i + 1) * r, last), 0)
    return prev, nxt


def _mod_kernel(c_ref, w_ref, b_ref, o_ref):
    c = c_ref[...]
    a = (c / (1.0 + jnp.exp(-c))).astype(BF16)
    o_ref[0] = jnp.dot(a, w_ref[0].astype(BF16), preferred_element_type=F32) + b_ref[0]


def _modulation(c8, mod_w, mod_b):
    depth, d, n = mod_w.shape
    tn = _tile(n, 1024)
    return pl.pallas_call(
        _mod_kernel,
        out_shape=jax.ShapeDtypeStruct((depth, MOD_ROWS, n), F32),
        grid=(depth, n // tn),
        in_specs=[
            pl.BlockSpec((MOD_ROWS, d), lambda l, j: (0, 0)),
            pl.BlockSpec((1, d, tn), lambda l, j: (l, 0, j)),
            pl.BlockSpec((1, 1, tn), lambda l, j: (l, 0, j)),
        ],
        out_specs=pl.BlockSpec((1, MOD_ROWS, tn), lambda l, j: (l, 0, j)),
        compiler_params=_cparams("arbitrary", "arbitrary"),
        name="modulation",
    )(c8, mod_w, mod_b.reshape(depth, 1, n))


def _fill_h(h_scr, x_ref, xp_ref, xn_ref, mod_ref, g_ref, sh_idx, sc_idx, tm):
    m = mod_ref[0]
    shift = m[sh_idx:sh_idx + 1]
    scale1p = 1.0 + m[sc_idx:sc_idx + 1]
    gain = g_ref[...]
    h_scr[0:HALO] = _modnorm(xp_ref[...], gain, scale1p, shift).astype(h_scr.dtype)
    h_scr[HALO:HALO + tm] = _modnorm(x_ref[...], gain, scale1p, shift).astype(h_scr.dtype)
    h_scr[HALO + tm:] = _modnorm(xn_ref[...], gain, scale1p, shift).astype(h_scr.dtype)


def _edge_keep(tps):
    i = pl.program_id(0) % tps
    return (i != 0).astype(F32), (i != tps - 1).astype(F32)


def _conv3_rows(a, keep_first, keep_last, cw, cb, tm):
    rows = tm + 2 * HALO
    a = jnp.concatenate([a[0:HALO] * keep_first, a[HALO:HALO + tm], a[HALO + tm:] * keep_last], axis=0)
    prev = pltpu.roll(a, 1, 0)
    nxt = pltpu.roll(a, rows - 1, 0)
    c = prev * cw[0:1] + a * cw[1:2] + nxt * cw[2:3] + cb
    return c[HALO:HALO + tm]


def _gelu_tanh(x):
    return 0.5 * x * (1.0 + jnp.tanh(0.7978845608028654 * (x + 0.044715 * (x * x * x))))


def _up_kernel(*refs, sh_idx, sc_idx, tm, tps, glu, has_bias):
    x_ref, xp_ref, xn_ref, mod_ref, g_ref, w_ref = refs[:6]
    k = 6
    wb_ref = b_ref = None
    if glu:
        wb_ref = refs[k]
        k += 1
    if has_bias:
        b_ref = refs[k]
        k += 1
    cw_ref, cb_ref, o_ref, h_scr = refs[k:k + 4]

    @pl.when(pl.program_id(1) == 0)
    def _():
        _fill_h(h_scr, x_ref, xp_ref, xn_ref, mod_ref, g_ref, sh_idx, sc_idx, tm)

    keep_first, keep_last = _edge_keep(tps)
    for s in range(o_ref.shape[1] // UP_SUB):
        sl = slice(s * UP_SUB, (s + 1) * UP_SUB)
        a = jnp.dot(h_scr[...], w_ref[:, sl], preferred_element_type=F32)
        if has_bias:
            a = a + b_ref[:, sl]
        c = _conv3_rows(a, keep_first, keep_last, cw_ref[:, sl], cb_ref[:, sl], tm)
        if glu:
            b = jnp.dot(h_scr[HALO:HALO + tm], wb_ref[:, sl], preferred_element_type=F32)
            c = _gelu_tanh(c) * b
        o_ref[:, sl] = c.astype(o_ref.dtype)


def _up(x, mod, gain, w, bias, cw, cb, *, layer, seq, sh_idx, sc_idx, glu, out_dtype):
    t_rows, d = x.shape
    n_out = cw.shape[-1]
    tm = _tile(seq, 1024)
    tn = _tile(n_out, UP_SUB if glu else 2 * UP_SUB)
    assert tn % UP_SUB == 0
    tps = seq // tm
    nj = n_out // tn
    prev, nxt = _halo_maps(tm, t_rows)
    in_specs = [
        pl.BlockSpec((tm, d), lambda i, j: (i, 0)),
        pl.BlockSpec((HALO, d), prev),
        pl.BlockSpec((HALO, d), nxt),
        pl.BlockSpec((1, 6, d), lambda i, j: (i // tps, 0, 0)),
        pl.BlockSpec((1, d), lambda i, j: (0, 0)),
        pl.BlockSpec((None, d, tn), lambda i, j: (layer, 0, j)),
    ]
    args = [x, x, x, mod, gain, w]
    if glu:
        in_specs.append(pl.BlockSpec((None, d, tn), lambda i, j: (layer, 0, j + nj)))
        args.append(w)
    if bias is not None:
        in_specs.append(pl.BlockSpec((1, tn), lambda i, j: (0, j)))
        args.append(bias)
    in_specs += [pl.BlockSpec((3, tn), lambda i, j: (0, j)), pl.BlockSpec((1, tn), lambda i, j: (0, j))]
    args += [cw, cb]
    return pl.pallas_call(
        functools.partial(_up_kernel, sh_idx=sh_idx, sc_idx=sc_idx, tm=tm, tps=tps, glu=glu,
                          has_bias=bias is not None),
        out_shape=jax.ShapeDtypeStruct((t_rows, n_out), out_dtype),
        grid=(t_rows // tm, nj),
        in_specs=in_specs,
        out_specs=pl.BlockSpec((tm, tn), lambda i, j: (i, j)),
        scratch_shapes=[pltpu.VMEM((tm + 2 * HALO, d), BF16)],
        compiler_params=_cparams("arbitrary", "arbitrary"),
        name="up_glu" if glu else "up_conv",
    )(*args)


def _down_kernel(g_ref, w_ref, x_ref, mod_ref, gain_ref, o_ref, acc_ref, *, gate_idx, nk):
    k = pl.program_id(1)
    part = jnp.dot(g_ref[...].astype(w_ref.dtype), w_ref[...], preferred_element_type=F32)

    @pl.when(k == 0)
    def _():
        acc_ref[...] = part

    @pl.when(k > 0)
    def _():
        acc_ref[...] += part

    @pl.when(k == nk - 1)
    def _():
        f = acc_ref[...]
        ms = jnp.mean(f * f, axis=-1, keepdims=True)
        y = f * lax.rsqrt(ms + EPS) * gain_ref[...]
        gate = mod_ref[0][gate_idx:gate_idx + 1]
        o_ref[...] = x_ref[...] + gate * y


def _down(g, w, x, mod, gain, *, layer, seq, gate_idx):
    t_rows, kdim = g.shape
    d = w.shape[2]
    tm = _tile(seq, 512)
    tk = _lane_tile(kdim, 2816)
    tps = seq // tm
    nk = kdim // tk
    return pl.pallas_call(
        functools.partial(_down_kernel, gate_idx=gate_idx, nk=nk),
        out_shape=jax.ShapeDtypeStruct((t_rows, d), F32),
        grid=(t_rows // tm, nk),
        in_specs=[
            pl.BlockSpec((tm, tk), lambda i, k: (i, k)),
            pl.BlockSpec((None, tk, d), lambda i, k: (layer, k, 0)),
            pl.BlockSpec((tm, d), lambda i, k: (i, 0)),
            pl.BlockSpec((1, 6, d), lambda i, k: (i // tps, 0, 0)),
            pl.BlockSpec((1, d), lambda i, k: (0, 0)),
        ],
        out_specs=pl.BlockSpec((tm, d), lambda i, k: (i, 0)),
        scratch_shapes=[pltpu.VMEM((tm, d), F32)],
        compiler_params=_cparams("arbitrary", "arbitrary"),
        name="down",
    )(g, w, x, mod, gain)


def _pool_kernel(x_ref, xp_ref, xn_ref, mod_ref, gpre_ref, gpost_ref, pw_ref, ps_ref, o_ref, h_scr,
                 *, tm, tps, seq):
    _fill_h(h_scr, x_ref, xp_ref, xn_ref, mod_ref, gpre_ref, 0, 1, tm)
    keep_first, keep_last = _edge_keep(tps)
    rows = tm + 2 * HALO
    d = x_ref.shape[1]
    cg = d // len(POOL_WINDOWS)
    pos = (pl.program_id(0) % tps) * tm + lax.broadcasted_iota(jnp.int32, (tm, 1), 0)
    ys = []
    ssq = jnp.zeros((tm, 1), F32)
    for g, win in enumerate(POOL_WINDOWS):
        half = win // 2
        sl = slice(g * cg, (g + 1) * cg)
        hg = jnp.concatenate([h_scr[0:HALO, sl] * keep_first, h_scr[HALO:HALO + tm, sl],
                              h_scr[HALO + tm:, sl] * keep_last], axis=0)
        p = hg
        s = 1
        while s < win:
            p = p + pltpu.roll(p, s, 0)
            s *= 2
        if half > 1:
            p = pltpu.roll(p, rows - (half - 1), 0)
        lo = jnp.maximum(pos - half, 0)
        hi = jnp.minimum(pos + (half - 1), seq - 1)
        cnt = (hi - lo + 1).astype(F32)
        pooled = p[HALO:HALO + tm] / cnt - hg[HALO:HALO + tm]
        y = jnp.dot(pooled.astype(BF16), pw_ref[g], preferred_element_type=F32) * ps_ref[:, sl]
        ssq = ssq + jnp.sum(y * y, axis=-1, keepdims=True)
        ys.append(y)
    inv = lax.rsqrt(ssq / d + EPS)
    gate = mod_ref[0][2:3]
    for g in range(len(POOL_WINDOWS)):
        sl = slice(g * cg, (g + 1) * cg)
        o_ref[:, sl] = x_ref[:, sl] + gate[:, sl] * (ys[g] * inv * gpost_ref[:, sl])


def _pool_layer(x, mod, gpre, gpost, pw, ps, *, seq):
    t_rows, d = x.shape
    tm = _tile(seq, 256)
    tps = seq // tm
    prev, nxt = _halo_maps(tm, t_rows)
    return pl.pallas_call(
        functools.partial(_pool_kernel, tm=tm, tps=tps, seq=seq),
        out_shape=jax.ShapeDtypeStruct((t_rows, d), F32),
        grid=(t_rows // tm,),
        in_specs=[
            pl.BlockSpec((tm, d), lambda i: (i, 0)),
            pl.BlockSpec((HALO, d), prev),
            pl.BlockSpec((HALO, d), nxt),
            pl.BlockSpec((1, 6, d), lambda i: (i // tps, 0, 0)),
            pl.BlockSpec((1, d), lambda i: (0, 0)),
            pl.BlockSpec((1, d), lambda i: (0, 0)),
            pl.BlockSpec(pw.shape, lambda i: (0, 0, 0)),
            pl.BlockSpec((1, d), lambda i: (0, 0)),
        ],
        out_specs=pl.BlockSpec((tm, d), lambda i: (i, 0)),
        scratch_shapes=[pltpu.VMEM((tm + 2 * HALO, d), F32)],
        compiler_params=_cparams("arbitrary"),
        name="pool_layer",
    )(x, x, x, mod, gpre, gpost, pw, ps)


def _rope_tables(seq):
    pos = jnp.arange(seq, dtype=jnp.int32)
    row = (pos // GRID_W).astype(F32)[:, None]
    col = (pos % GRID_W).astype(F32)[:, None]
    axis_dim = HEAD_DIM // 2
    inv = ROPE_THETA ** (-jnp.arange(0, axis_dim, 2, dtype=F32) / axis_dim)
    ar, ac = row * inv, col * inv
    cos = jnp.concatenate([jnp.cos(ar), jnp.cos(ar), jnp.cos(ac), jnp.cos(ac)], axis=-1)
    sin = jnp.concatenate([-jnp.sin(ar), jnp.sin(ar), -jnp.sin(ac), jnp.sin(ac)], axis=-1)
    return cos, sin


def _qkv_kernel(x_ref, mod_ref, g_ref, w0_ref, w1_ref, hg_ref, cos_ref, sin_ref, q_ref, k_ref, vt_ref, h_scr, *,
                n_q_steps):
    j = pl.program_id(1)
    tn = k_ref.shape[1]

    @pl.when(j == 0)
    def _():
        m = mod_ref[0]
        h_scr[...] = _modnorm(x_ref[...], g_ref[...], 1.0 + m[1:2], m[0:1]).astype(BF16)

    def proj(t):
        return jnp.dot(h_scr[...], (w0_ref, w1_ref)[t][...], preferred_element_type=F32)

    def norm_rope(a, gain, out_scale, o_ref, col0):
        cos = cos_ref[...]
        sin = sin_ref[...]
        quarter = HEAD_DIM // 4
        lane = lax.broadcasted_iota(jnp.int32, (1, HEAD_DIM), 1)
        low = (lane % (2 * quarter)) < quarter
        for h in range(tn // HEAD_DIM):
            v = a[:, h * HEAD_DIM:(h + 1) * HEAD_DIM]
            ms = jnp.mean(v * v, axis=-1, keepdims=True)
            v = v * lax.rsqrt(ms + EPS) * gain
            swapped = jnp.where(low, pltpu.roll(v, HEAD_DIM - quarter, 1), pltpu.roll(v, quarter, 1))
            o_ref[:, col0 + h * HEAD_DIM:col0 + (h + 1) * HEAD_DIM] = (
                (v * cos + swapped * sin) * out_scale).astype(o_ref.dtype)

    @pl.when(j < n_q_steps)
    def _():
        for t in range(2):
            norm_rope(proj(t), hg_ref[0:1], HEAD_DIM ** -0.5 * LOG2E, q_ref, t * tn)

    @pl.when(j == n_q_steps)
    def _():
        norm_rope(proj(0), hg_ref[1:2], 1.0, k_ref, 0)
        v = proj(1)
        for c in range(vt_ref.shape[0]):
            vt_ref[c] = v[c * FLASH_TK:(c + 1) * FLASH_TK].T.astype(vt_ref.dtype)


def _qkv(x, mod, gain, w, head_gains, cos, sin, *, seq, n_kv):
    t_rows, d = x.shape
    n_out = w.shape[1]
    tm = _tile(seq, 512)
    tn = HEAD_DIM * n_kv
    tps = seq // tm
    n_q_steps = d // (2 * tn)
    assert tm % FLASH_TK == 0 and n_out == d + 2 * tn and d % (2 * tn) == 0
    return pl.pallas_call(
        functools.partial(_qkv_kernel, n_q_steps=n_q_steps),
        out_shape=(jax.ShapeDtypeStruct((t_rows, d), BF16), jax.ShapeDtypeStruct((t_rows, tn), BF16),
                   jax.ShapeDtypeStruct((t_rows // FLASH_TK, tn, FLASH_TK), BF16)),
        grid=(t_rows // tm, n_q_steps + 1),
        in_specs=[
            pl.BlockSpec((tm, d), lambda i, j: (i, 0)),
            pl.BlockSpec((1, 6, d), lambda i, j: (i // tps, 0, 0)),
            pl.BlockSpec((1, d), lambda i, j: (0, 0)),
            pl.BlockSpec((d, tn), lambda i, j: (0, 2 * j)),
            pl.BlockSpec((d, tn), lambda i, j: (0, 2 * j + 1)),
            pl.BlockSpec((2, HEAD_DIM), lambda i, j: (0, 0)),
            pl.BlockSpec((tm, HEAD_DIM), lambda i, j: (i % tps, 0)),
            pl.BlockSpec((tm, HEAD_DIM), lambda i, j: (i % tps, 0)),
        ],
        out_specs=(
            pl.BlockSpec((tm, 2 * tn), lambda i, j: (i, jnp.minimum(j, n_q_steps - 1))),
            pl.BlockSpec((tm, tn), lambda i, j: (i, 0)),
            pl.BlockSpec((tm // FLASH_TK, tn, FLASH_TK), lambda i, j: (i, 0, 0)),
        ),
        scratch_shapes=[pltpu.VMEM((tm, d), BF16)],
        compiler_params=_cparams("arbitrary", "arbitrary"),
        name="qkv_rope",
    )(x, mod, gain, w, w, head_gains, cos, sin)


def _flash_kernel(q_ref, k_ref, vt_ref, o_ref, acc_ref, s_ref, *, tq, nk):
    q = q_ref[...]
    qs = jnp.concatenate([q[:, h * HEAD_DIM:(h + 1) * HEAD_DIM] for h in range(GQA_GROUP)], axis=0)
    ones = jnp.ones((ONES_ROWS, FLASH_TK), BF16)
    acc_ref[...] = jnp.zeros_like(acc_ref)

    def scores(kk, slot):
        start = pl.multiple_of(kk * FLASH_TK, FLASH_TK)
        s_ref[slot] = lax.dot_general(k_ref[pl.ds(start, FLASH_TK), :], qs, (((1,), (1,)), ((), ())),
                                      preferred_element_type=F32)

    def accumulate(kk, slot, m):
        st = s_ref[slot]
        m_new = jnp.maximum(m, jnp.max(st, axis=0, keepdims=True))
        alpha = jnp.exp2(m - m_new)
        pt = jnp.exp2(st - m_new).astype(BF16)
        va = jnp.concatenate([vt_ref[kk], ones], axis=0)
        acc_ref[...] = alpha * acc_ref[...] + jnp.dot(va, pt, preferred_element_type=F32)
        return m_new

    scores(0, 0)
    unroll = 4 if nk % 4 == 0 else 2

    def group(i, m):
        k0 = unroll * i
        for u in range(unroll):
            nxt = k0 + u + 1
            scores(nxt if u + 1 < unroll else jnp.minimum(nxt, nk - 1), (u + 1) % 2)
            m = accumulate(k0 + u, u % 2, m)
        return m

    lax.fori_loop(0, nk // unroll, group, jnp.full((1, GQA_GROUP * tq), -jnp.inf, F32))
    acc = acc_ref[...]
    ot = acc[:HEAD_DIM] / acc[HEAD_DIM:HEAD_DIM + 1]
    for h in range(GQA_GROUP):
        o_ref[:, h * HEAD_DIM:(h + 1) * HEAD_DIM] = ot[:, h * tq:(h + 1) * tq].T.astype(o_ref.dtype)


def _flash(q, k, vt, *, batch, seq, n_kv):
    t_rows, d = q.shape
    tq = _tile(seq, 512)
    nqb = seq // tq
    nk = seq // FLASH_TK
    assert nk % 2 == 0
    gw = GQA_GROUP * HEAD_DIM
    return pl.pallas_call(
        functools.partial(_flash_kernel, tq=tq, nk=nk),
        out_shape=jax.ShapeDtypeStruct((t_rows, d), BF16),
        grid=(batch, n_kv, nqb),
        in_specs=[
            pl.BlockSpec((tq, gw), lambda b, kv, qi: (b * nqb + qi, kv)),
            pl.BlockSpec((seq, HEAD_DIM), lambda b, kv, qi: (b, kv)),
            pl.BlockSpec((nk, HEAD_DIM, FLASH_TK), lambda b, kv, qi: (b, kv, 0)),
        ],
        out_specs=pl.BlockSpec((tq, gw), lambda b, kv, qi: (b * nqb + qi, kv)),
        scratch_shapes=[pltpu.VMEM((HEAD_DIM + ONES_ROWS, GQA_GROUP * tq), F32),
                        pltpu.VMEM((2, FLASH_TK, GQA_GROUP * tq), F32)],
        compiler_params=_cparams("arbitrary", "arbitrary", "arbitrary"),
        name="flash_attention",
    )(q, k, vt)


def _dft_sizes(seq):
    n = 2 * seq
    n2 = DFT_N2_MAX
    while n // n2 < DFT_N1_MIN:
        n2 //= 2
    return n, n // n2, n2


def _dft_tables(seq):
    n, n1, n2 = _dft_sizes(seq)
    h2 = n2 // 2
    k2 = jnp.arange(h2, dtype=jnp.int32)
    m2 = jnp.arange(h2, dtype=jnp.int32)
    idx = (m2[None, :] * (2 * k2[:, None] + 1)) % (2 * n2)
    th = idx.astype(F32) * (2.0 * math.pi / (2 * n2))
    m1 = jnp.concatenate([jnp.cos(th), -jnp.sin(th)], axis=0)
    a1 = jnp.arange(n1, dtype=jnp.int32)
    k1 = jnp.arange(n1, dtype=jnp.int32)
    freq = 2 * n2 * k1[None, :, None] + 2 * k2[:, None, None] + 1
    idx = (a1[None, None, :] * freq) % (2 * n)
    ph = idx.astype(F32) * (2.0 * math.pi / (2 * n))
    c, s = jnp.cos(ph), jnp.sin(ph)
    gb = jnp.concatenate([jnp.concatenate([c, s], axis=2), jnp.concatenate([-s, c], axis=2)], axis=1)
    return m1, gb


def _filt_kernel(z_ref, w1_ref, b1_ref, w2_ref, b2_ref, fr_ref, w3_ref, dl_ref, o_ref):
    z = z_ref[...]
    fr = fr_ref[...]
    a = jnp.sin(fr * (_dot3(z, w1_ref[...]) + b1_ref[...]))
    a = jnp.sin(fr * (_dot3(a, w2_ref[...]) + b2_ref[...]))
    decay = jnp.exp(-z[:, 0:1] * dl_ref[...]) + HYENA_MOD_SHIFT
    d = decay.shape[1]
    row = pl.program_id(0) * z.shape[0] + lax.broadcasted_iota(jnp.int32, (z.shape[0], 1), 0)
    for c in range(w3_ref.shape[1] // d):
        f = _dot3(a, w3_ref[:, c * d:(c + 1) * d]) * decay
        o_ref[:, c * d:(c + 1) * d] = jnp.where(row == 0, 0.0, f) if c % 2 == 1 else f


def _filters(z, w1, b1, w2, b2, fr, w3, deltas):
    seq = z.shape[0]
    d = deltas.shape[1]
    tl = _tile(seq, 128)
    hid = w2.shape[0]
    const = lambda i: (0, 0)
    return pl.pallas_call(
        _filt_kernel,
        out_shape=jax.ShapeDtypeStruct((seq, w3.shape[1]), F32),
        grid=(seq // tl,),
        in_specs=[
            pl.BlockSpec((tl, HYENA_EMB_PAD), lambda i: (i, 0)),
            pl.BlockSpec((HYENA_EMB_PAD, hid), const),
            pl.BlockSpec((1, hid), const),
            pl.BlockSpec((hid, hid), const),
            pl.BlockSpec((1, hid), const),
            pl.BlockSpec((1, hid), const),
            pl.BlockSpec(w3.shape, const),
            pl.BlockSpec((1, d), const),
        ],
        out_specs=pl.BlockSpec((tl, w3.shape[1]), lambda i: (i, 0)),
        compiler_params=_cparams("arbitrary"),
        name="hyena_filters",
    )(z, w1, b1, w2, b2, fr, w3, deltas)


def _dft1_kernel(x_ref, m_ref, o_ref, y_scr):
    xt = pltpu.einshape("hjd->jhd", x_ref[0])
    m = m_ref[...]
    for j in range(SUBLANES):
        y_scr[j] = _dft_dot(m, xt[j])
    yt = pltpu.einshape("jnd->njd", y_scr[...])
    h2 = yt.shape[0] // 2
    o_ref[0, 0] = yt[:h2]
    o_ref[0, 1] = yt[h2:]


def _dft1(x4, m1b, *, td, n_blocks, col_map):
    b, h2, n1, _ = x4.shape
    n2 = m1b.shape[0]
    return pl.pallas_call(
        _dft1_kernel,
        out_shape=jax.ShapeDtypeStruct((b, 2, h2, n1, n_blocks * td), F32),
        grid=(b, n1 // SUBLANES, n_blocks),
        in_specs=[
            pl.BlockSpec((1, h2, SUBLANES, td), lambda bi, g, c: (bi, 0, g, col_map(c))),
            pl.BlockSpec(m1b.shape, lambda bi, g, c: (0, 0)),
        ],
        out_specs=pl.BlockSpec((1, 2, h2, SUBLANES, td), lambda bi, g, c: (bi, 0, 0, g, c)),
        scratch_shapes=[pltpu.VMEM((SUBLANES, n2, td), F32)],
        compiler_params=_cparams("arbitrary", "arbitrary", "arbitrary"),
        name="dft_stage1",
    )(x4, m1b)


def _fspec_kernel(af_ref, ab_ref, g_ref, o_ref, *, kb, n1):
    xf = [_dft_dot(g_ref[kk], jnp.concatenate([af_ref[0, kk], af_ref[1, kk]], axis=0)) for kk in range(kb)]
    xb = [_dft_dot(g_ref[kk], jnp.concatenate([ab_ref[0, kk], ab_ref[1, kk]], axis=0)) for kk in range(kb)]
    for kk in range(kb):
        o_ref[0, 0, kk] = xf[kk][:n1] + xb[kk][:n1]
        o_ref[0, 1, kk] = xf[kk][n1:] - xb[kk][n1:]


def _filter_spectrum(af, gbb, *, d, kb, td):
    _, h2, n1, w = af.shape
    ndt = d // td
    norder = w // (2 * d)
    return pl.pallas_call(
        functools.partial(_fspec_kernel, kb=kb, n1=n1),
        out_shape=jax.ShapeDtypeStruct((norder, 2, h2, n1, d), F32),
        grid=(norder, h2 // kb, ndt),
        in_specs=[
            pl.BlockSpec((2, kb, n1, td), lambda o, k, t: (0, k, 0, (2 * o) * ndt + t)),
            pl.BlockSpec((2, kb, n1, td), lambda o, k, t: (0, k, 0, (2 * o + 1) * ndt + t)),
            pl.BlockSpec((kb, 2 * n1, 2 * n1), lambda o, k, t: (k, 0, 0)),
        ],
        out_specs=pl.BlockSpec((1, 2, kb, n1, td), lambda o, k, t: (o, 0, k, 0, t)),
        compiler_params=_cparams("arbitrary", "arbitrary", "arbitrary"),
        name="hyena_filter_spectrum",
    )(af, af, gbb)


def _cmid_kernel(a_ref, ks_ref, g_ref, gt_ref, o_ref, *, kb, n1):
    xs = [_dft_dot(g_ref[kk], jnp.concatenate([a_ref[0, 0, kk], a_ref[0, 1, kk]], axis=0)) for kk in range(kb)]
    ys = []
    for kk in range(kb):
        xr, xi = xs[kk][:n1], xs[kk][n1:]
        kr, ki = ks_ref[0, 0, kk], ks_ref[0, 1, kk]
        ys.append(jnp.concatenate([xr * kr - xi * ki, xr * ki + xi * kr], axis=0))
    bps = [_dft_dot(gt_ref[kk], ys[kk]) for kk in range(kb)]
    for kk in range(kb):
        o_ref[0, 0, kk] = bps[kk][:n1]
        o_ref[0, 1, kk] = bps[kk][n1:]


def _conv_mid(a, kspec, order, gbb, gtb, *, kb, td):
    b, _, h2, n1, d = a.shape
    return pl.pallas_call(
        functools.partial(_cmid_kernel, kb=kb, n1=n1),
        out_shape=jax.ShapeDtypeStruct(a.shape, F32),
        grid=(h2 // kb, d // td, b),
        in_specs=[
            pl.BlockSpec((1, 2, kb, n1, td), lambda k, t, bi: (bi, 0, k, 0, t)),
            pl.BlockSpec((1, 2, kb, n1, td), lambda k, t, bi: (order, 0, k, 0, t)),
            pl.BlockSpec((kb, 2 * n1, 2 * n1), lambda k, t, bi: (k, 0, 0)),
            pl.BlockSpec((kb, 2 * n1, 2 * n1), lambda k, t, bi: (k, 0, 0)),
        ],
        out_specs=pl.BlockSpec((1, 2, kb, n1, td), lambda k, t, bi: (bi, 0, k, 0, t)),
        compiler_params=_cparams("arbitrary", "arbitrary", "arbitrary"),
        name="hyena_conv_mid",
    )(a, kspec, gbb, gtb)


def _idft1_kernel(bp_ref, m_ref, gate_ref, zz_ref, skip_ref, o_ref, y_scr):
    bp = jnp.concatenate([bp_ref[0, 0], bp_ref[0, 1]], axis=0)
    bt = pltpu.einshape("njd->jnd", bp)
    m = m_ref[...]
    for j in range(SUBLANES):
        y_scr[j] = _dft_dot(m, bt[j])
    y = pltpu.einshape("jhd->hjd", y_scr[...])
    o_ref[0] = (gate_ref[0] * (y + zz_ref[0] * skip_ref[...])).astype(o_ref.dtype)


def _idft1_gate(bp, minvb, u4, zz4, skip, *, td, gate_part, zz_part):
    b, _, h2, n1, d = bp.shape
    nt = d // td
    return pl.pallas_call(
        _idft1_kernel,
        out_shape=jax.ShapeDtypeStruct((b, h2, n1, d), F32),
        grid=(b, n1 // SUBLANES, nt),
        in_specs=[
            pl.BlockSpec((1, 2, h2, SUBLANES, td), lambda bi, g, c: (bi, 0, 0, g, c)),
            pl.BlockSpec(minvb.shape, lambda bi, g, c: (0, 0)),
            pl.BlockSpec((1, h2, SUBLANES, td), lambda bi, g, c: (bi, 0, g, gate_part * nt + c)),
            pl.BlockSpec((1, h2, SUBLANES, td), lambda bi, g, c: (bi, 0, g, zz_part * nt + c)),
            pl.BlockSpec((1, td), lambda bi, g, c: (0, c)),
        ],
        out_specs=pl.BlockSpec((1, h2, SUBLANES, td), lambda bi, g, c: (bi, 0, g, c)),
        scratch_shapes=[pltpu.VMEM((SUBLANES, h2, td), F32)],
        compiler_params=_cparams("arbitrary", "arbitrary", "arbitrary"),
        name="idft_stage1_gate",
    )(bp, minvb, u4, zz4, skip)


def _hyena_features(seq, d):
    t = jnp.linspace(0.0, 1.0, seq, dtype=F32)[:, None]
    w = 2.0 * math.pi * jnp.arange(seq, dtype=F32)[:, None] / seq
    f = jnp.linspace(1e-4, HYENA_BANDS - 1, HYENA_BANDS, dtype=F32)[None, :]
    z = jnp.concatenate([t, jnp.cos(f * w), -jnp.sin(f * w)], axis=-1)
    z = jnp.pad(z, ((0, 0), (0, HYENA_EMB_PAD - HYENA_EMB)))
    deltas = jnp.abs(jnp.linspace(HYENA_MIN_DECAY, HYENA_MAX_DECAY, d, dtype=F32))[None, :]
    return z, deltas


def _hyena_mixer(x, mod, gain, p, *, batch, seq):
    t_rows, d = x.shape
    n, n1, n2 = _dft_sizes(seq)
    h2 = n2 // 2
    u = _up(x, mod, gain, p["w_in"][None], p["b_in"], p["conv_w"], p["conv_b"], layer=0, seq=seq, sh_idx=0,
            sc_idx=1, glu=False, out_dtype=F32)

    z, deltas = _hyena_features(seq, d)
    filt = _filters(z, p["f_w1"], p["f_b1"], p["f_w2"], p["f_b2"], p["f_freq"], p["f_w3"], deltas)

    m1, gb = _dft_tables(seq)
    m1b = m1.astype(BF16)
    minvb = (m1.T * (2.0 / n)).astype(BF16)
    gbb = gb.astype(BF16)
    gtb = jnp.swapaxes(gb, 1, 2).astype(BF16)
    kb = max(1, 256 // n1)
    td = _tile(d, 1024)
    ts = _tile(d, 256)
    nts = d // ts

    ncf = filt.shape[1]
    af = _dft1(filt.reshape(1, h2, n1, ncf), m1b, td=ts, n_blocks=ncf // ts, col_map=lambda c: c)
    kspec = _filter_spectrum(af[0], gbb, d=d, kb=kb, td=td)

    u4 = u.reshape(batch, h2, n1, 3 * d)
    zz4 = u4
    for o in range(HYENA_ORDER):
        a = _dft1(zz4, m1b, td=ts, n_blocks=nts, col_map=lambda c: c)
        bp = _conv_mid(a, kspec, o, gbb, gtb, kb=kb, td=td)
        zz4 = _idft1_gate(bp, minvb, u4, zz4, p["skip"][o:o + 1], td=ts, gate_part=1 + o, zz_part=0)
    return zz4.reshape(t_rows, d)


def _trunk(x3, modall, row0, p):
    batch, seq, d = x3.shape
    t_rows = batch * seq
    x = x3.reshape(t_rows, d)
    n_kv = d // HEAD_DIM // GQA_GROUP
    depth = modall.shape[0]
    for i in range(depth):
        mod = modall[i, row0:row0 + batch].reshape(batch, 6, d)
        kind, j = i % 3, i // 3
        g_pre, g_post = p["norm_mix_pre"][i:i + 1], p["norm_mix_post"][i:i + 1]
        if kind == 0:
            x = _pool_layer(x, mod, g_pre, g_post, p["pool_w"][j], p["pool_scale"][j:j + 1], seq=seq)
        elif kind == 1:
            hp = {k: v[j] for k, v in p["hy"].items()}
            zz = _hyena_mixer(x, mod, g_pre, hp, batch=batch, seq=seq)
            x = _down(zz, p["hy"]["w_out"], x, mod, g_post, layer=j, seq=seq, gate_idx=2)
        else:
            cos, sin = _rope_tables(seq)
            gains = jnp.stack([p["at_q_gain"][j], p["at_k_gain"][j]], axis=0)
            q, k, vt = _qkv(x, mod, g_pre, p["at_w_qkv"][j], gains, cos, sin, seq=seq, n_kv=n_kv)
            o = _flash(q, k, vt, batch=batch, seq=seq, n_kv=n_kv)
            x = _down(o, p["at_w_o"], x, mod, g_post, layer=j, seq=seq, gate_idx=2)
        g = _up(x, mod, p["norm_ffn_pre"][i:i + 1], p["ffn_w_up"], None, p["ffn_conv_w"][i],
                p["ffn_conv_b"][i:i + 1], layer=i, seq=seq, sh_idx=3, sc_idx=4, glu=True, out_dtype=BF16)
        x = _down(g, p["ffn_w_down"], x, mod, p["norm_ffn_post"][i:i + 1], layer=i, seq=seq, gate_idx=5)
    return x.reshape(batch, seq, d)


def kernel(x_prompt, x_sample, c_prompt, c_sample, mod_w, mod_b, norm_mix_pre, norm_mix_post, norm_ffn_pre,
           norm_ffn_post, ffn_w_up, ffn_conv_w, ffn_conv_b, ffn_w_down, pool_w, pool_scale, hy_w_in, hy_b_in,
           hy_conv_w, hy_conv_b, hy_f_w1, hy_f_b1, hy_f_w2, hy_f_b2, hy_f_w3, hy_f_freq, hy_skip, hy_w_out,
           at_w_qkv, at_q_gain, at_k_gain, at_w_o):
    nb = c_prompt.shape[0]
    ns = c_sample.shape[0]
    assert nb + ns <= MOD_ROWS
    c8 = jnp.concatenate([c_prompt, c_sample, jnp.zeros((MOD_ROWS - nb - ns, c_prompt.shape[1]), F32)], axis=0)
    modall = _modulation(c8, mod_w, mod_b)
    row = lambda v: v[:, None, :]
    p = dict(
        norm_mix_pre=norm_mix_pre, norm_mix_post=norm_mix_post, norm_ffn_pre=norm_ffn_pre,
        norm_ffn_post=norm_ffn_post,
        ffn_w_up=ffn_w_up.astype(BF16), ffn_conv_w=ffn_conv_w, ffn_conv_b=ffn_conv_b,
        ffn_w_down=ffn_w_down.astype(BF16),
        pool_w=pool_w.astype(BF16), pool_scale=pool_scale,
        hy=dict(
            w_in=hy_w_in.astype(BF16), b_in=row(hy_b_in), conv_w=hy_conv_w, conv_b=row(hy_conv_b),
            f_w1=jnp.pad(hy_f_w1, ((0, 0), (0, HYENA_EMB_PAD - HYENA_EMB), (0, 0))), f_b1=row(hy_f_b1),
            f_w2=hy_f_w2, f_b2=row(hy_f_b2), f_w3=hy_f_w3, f_freq=row(hy_f_freq), skip=hy_skip,
            w_out=hy_w_out.astype(BF16)),
        at_w_qkv=at_w_qkv.astype(BF16), at_q_gain=at_q_gain, at_k_gain=at_k_gain, at_w_o=at_w_o.astype(BF16),
    )
    y_prompt = _trunk(x_prompt, modall, 0, p)
    y_sample = _trunk(x_sample, modall, nb, p)
    return (y_prompt, y_sample)
```

```python
import functools
import math

import jax
import jax.numpy as jnp
from jax import lax
from jax.experimental import pallas as pl
from jax.experimental.pallas import tpu as pltpu

EPS = 1e-6
HEAD_DIM = 128
GQA_GROUP = 4
GRID_W = 64
ROPE_THETA = 10000.0
POOL_WINDOWS = (2, 4, 8, 16)
HYENA_ORDER = 2
HYENA_EMB = 33
HYENA_BANDS = (HYENA_EMB - 1) // 2
HYENA_EMB_PAD = 64
HYENA_FAST_DECAY = 0.3
HYENA_SLOW_DECAY = 1.5
HYENA_TARGET = 1e-2
HYENA_MOD_SHIFT = 0.05
HYENA_MIN_DECAY = math.log(HYENA_TARGET) / HYENA_SLOW_DECAY
HYENA_MAX_DECAY = math.log(HYENA_TARGET) / HYENA_FAST_DECAY

HALO = 16
MOD_ROWS = 8
DFT_N2_MAX = 512
DFT_N1_MIN = 16
DFT_ROWS = 16
IDFT_ROWS = 8
LANES = 128
UP_SUB = 512
FLASH_TK = 512
ONES_ROWS = 16
LOG2E = 1.4426950408889634
VMEM_LIMIT = 58 * 1024 * 1024

F32 = jnp.float32
BF16 = jnp.bfloat16


def _cparams(*sem):
    return pltpu.CompilerParams(dimension_semantics=sem, vmem_limit_bytes=VMEM_LIMIT)


def _tile(dim, pref):
    t = min(dim, pref)
    while dim % t:
        t //= 2
    return t


def _lane_tile(dim, cap):
    units = dim // LANES
    best = max(u for u in range(1, units + 1) if units % u == 0 and u * LANES <= max(cap, LANES))
    return best * LANES


def _split3(x):
    hi = x.astype(BF16)
    lo = (x - hi.astype(F32)).astype(BF16)
    return hi, lo


def _dot3(a, b):
    ah, al = _split3(a)
    bh, bl = _split3(b)
    d = functools.partial(jnp.dot, preferred_element_type=F32)
    return d(ah, bh) + (d(ah, bl) + d(al, bh))


def _dft_dot(m, x):
    return jnp.dot(m, x.astype(BF16), preferred_element_type=F32)


def _modnorm(xv, gain, scale1p, shift):
    ms = jnp.mean(xv * xv, axis=-1, keepdims=True)
    return xv * lax.rsqrt(ms + EPS) * gain * scale1p + shift


def _halo_maps(tm, t_rows):
    r = tm // HALO
    last = t_rows // HALO - 1
    prev = lambda i, *_: (jnp.maximum(i * r - 1, 0), 0)
    nxt = lambda i, *_: (jnp.minimum((i + 1) * r, last), 0)
    return prev, nxt


def _mod_kernel(c_ref, w_ref, b_ref, o_ref):
    c = c_ref[...]
    a = (c / (1.0 + jnp.exp(-c))).astype(BF16)
    o_ref[0] = jnp.dot(a, w_ref[0].astype(BF16), preferred_element_type=F32) + b_ref[0]


def _modulation(c8, mod_w, mod_b):
    depth, d, n = mod_w.shape
    tn = _tile(n, 1024)
    return pl.pallas_call(
        _mod_kernel,
        out_shape=jax.ShapeDtypeStruct((depth, MOD_ROWS, n), F32),
        grid=(depth, n // tn),
        in_specs=[
            pl.BlockSpec((MOD_ROWS, d), lambda l, j: (0, 0)),
            pl.BlockSpec((1, d, tn), lambda l, j: (l, 0, j)),
            pl.BlockSpec((1, 1, tn), lambda l, j: (l, 0, j)),
        ],
        out_specs=pl.BlockSpec((1, MOD_ROWS, tn), lambda l, j: (l, 0, j)),
        compiler_params=_cparams("arbitrary", "arbitrary"),
        name="modulation",
    )(c8, mod_w, mod_b.reshape(depth, 1, n))


def _fill_h(h_scr, x_ref, xp_ref, xn_ref, mod_ref, g_ref, sh_idx, sc_idx, tm):
    m = mod_ref[0]
    shift = m[sh_idx:sh_idx + 1]
    scale1p = 1.0 + m[sc_idx:sc_idx + 1]
    gain = g_ref[...]
    h_scr[0:HALO] = _modnorm(xp_ref[...], gain, scale1p, shift).astype(h_scr.dtype)
    h_scr[HALO:HALO + tm] = _modnorm(x_ref[...], gain, scale1p, shift).astype(h_scr.dtype)
    h_scr[HALO + tm:] = _modnorm(xn_ref[...], gain, scale1p, shift).astype(h_scr.dtype)


def _edge_keep(tps):
    i = pl.program_id(0) % tps
    return (i != 0).astype(F32), (i != tps - 1).astype(F32)


def _conv3_rows(a, keep_first, keep_last, cw, cb, tm):
    rows = tm + 2 * HALO
    a = jnp.concatenate([a[0:HALO] * keep_first, a[HALO:HALO + tm], a[HALO + tm:] * keep_last], axis=0)
    prev = pltpu.roll(a, 1, 0)
    nxt = pltpu.roll(a, rows - 1, 0)
    c = prev * cw[0:1] + a * cw[1:2] + nxt * cw[2:3] + cb
    return c[HALO:HALO + tm]


def _gelu_tanh(x):
    return 0.5 * x * (1.0 + jnp.tanh(0.7978845608028654 * (x + 0.044715 * (x * x * x))))


def _up_kernel(*refs, sh_idx, sc_idx, tm, tps, glu, has_bias, n_chunks):
    x_ref, xp_ref, xn_ref, mod_ref, g_ref, w_ref = refs[:6]
    k = 6
    wb_ref = b_ref = None
    if glu:
        wb_ref = refs[k]
        k += 1
    if has_bias:
        b_ref = refs[k]
        k += 1
    cw_ref, cb_ref, o_ref, h_scr, hn_scr = refs[k:k + 5]
    i = pl.program_id(0)
    j = pl.program_id(1)

    @pl.when(jnp.logical_and(j == 0, i == 0))
    def _():
        _fill_h(h_scr, x_ref, xp_ref, xn_ref, mod_ref, g_ref, sh_idx, sc_idx, tm)

    @pl.when(jnp.logical_and(j == 0, i > 0))
    def _():
        h_scr[...] = hn_scr[...]

    m = mod_ref[0]
    shift = m[sh_idx:sh_idx + 1]
    scale1p = 1.0 + m[sc_idx:sc_idx + 1]
    gain = g_ref[...]

    @pl.when(j == 1)
    def _():
        hn_scr[0:HALO] = _modnorm(xp_ref[...], gain, scale1p, shift).astype(hn_scr.dtype)
        hn_scr[HALO + tm:] = _modnorm(xn_ref[...], gain, scale1p, shift).astype(hn_scr.dtype)

    rc = tm // n_chunks
    chunk = jnp.clip(j - 1, 0, n_chunks - 1)
    r0 = pl.multiple_of(chunk * rc, HALO)
    hn_scr[pl.ds(pl.multiple_of(HALO + r0, HALO), rc), :] = _modnorm(
        x_ref[pl.ds(r0, rc), :], gain, scale1p, shift).astype(hn_scr.dtype)

    keep_first, keep_last = _edge_keep(tps)
    for s in range(o_ref.shape[1] // UP_SUB):
        sl = slice(s * UP_SUB, (s + 1) * UP_SUB)
        a = jnp.dot(h_scr[...], w_ref[:, sl], preferred_element_type=F32)
        if has_bias:
            a = a + b_ref[:, sl]
        c = _conv3_rows(a, keep_first, keep_last, cw_ref[:, sl], cb_ref[:, sl], tm)
        if glu:
            b = jnp.dot(h_scr[HALO:HALO + tm], wb_ref[:, sl], preferred_element_type=F32)
            c = _gelu_tanh(c) * b
        o_ref[:, sl] = c.astype(o_ref.dtype)


def _up(x, mod, gain, w, bias, cw, cb, *, layer, seq, sh_idx, sc_idx, glu, out_dtype):
    t_rows, d = x.shape
    n_out = cw.shape[-1]
    tm = _tile(seq, 1024)
    tn = _tile(n_out, UP_SUB if glu else 2 * UP_SUB)
    assert tn % UP_SUB == 0
    tps = seq // tm
    nj = n_out // tn
    n_tiles = t_rows // tm
    assert nj >= 2
    n_chunks = 1
    while 2 * n_chunks <= nj - 1 and tm % (2 * n_chunks * HALO) == 0:
        n_chunks *= 2
    r = tm // HALO

    def ahead(i, j):
        return jnp.minimum(i + jnp.where(jnp.logical_and(i == 0, j == 0), 0, 1), n_tiles - 1)

    in_specs = [
        pl.BlockSpec((tm, d), lambda i, j: (ahead(i, j), 0)),
        pl.BlockSpec((HALO, d), lambda i, j: (jnp.maximum(ahead(i, j) * r - 1, 0), 0)),
        pl.BlockSpec((HALO, d), lambda i, j: (jnp.minimum((ahead(i, j) + 1) * r, t_rows // HALO - 1), 0)),
        pl.BlockSpec((1, 6, d), lambda i, j: (ahead(i, j) // tps, 0, 0)),
        pl.BlockSpec((1, d), lambda i, j: (0, 0)),
        pl.BlockSpec((None, d, tn), lambda i, j: (layer, 0, j)),
    ]
    args = [x, x, x, mod, gain, w]
    if glu:
        in_specs.append(pl.BlockSpec((None, d, tn), lambda i, j: (layer, 0, j + nj)))
        args.append(w)
    if bias is not None:
        in_specs.append(pl.BlockSpec((1, tn), lambda i, j: (0, j)))
        args.append(bias)
    in_specs += [pl.BlockSpec((3, tn), lambda i, j: (0, j)), pl.BlockSpec((1, tn), lambda i, j: (0, j))]
    args += [cw, cb]
    return pl.pallas_call(
        functools.partial(_up_kernel, sh_idx=sh_idx, sc_idx=sc_idx, tm=tm, tps=tps, glu=glu,
                          has_bias=bias is not None, n_chunks=n_chunks),
        out_shape=jax.ShapeDtypeStruct((t_rows, n_out), out_dtype),
        grid=(n_tiles, nj),
        in_specs=in_specs,
        out_specs=pl.BlockSpec((tm, tn), lambda i, j: (i, j)),
        scratch_shapes=[pltpu.VMEM((tm + 2 * HALO, d), BF16), pltpu.VMEM((tm + 2 * HALO, d), BF16)],
        compiler_params=_cparams("arbitrary", "arbitrary"),
        name="up_glu" if glu else "up_conv",
    )(*args)


def _down_kernel(g_ref, w_ref, x_ref, mod_ref, gain_ref, o_ref, acc_ref, *, gate_idx, nk):
    k = pl.program_id(1)
    part = jnp.dot(g_ref[...].astype(w_ref.dtype), w_ref[...], preferred_element_type=F32)

    @pl.when(k == 0)
    def _():
        acc_ref[...] = part

    @pl.when(k > 0)
    def _():
        acc_ref[...] += part

    @pl.when(k == nk - 1)
    def _():
        f = acc_ref[...]
        ms = jnp.mean(f * f, axis=-1, keepdims=True)
        y = f * lax.rsqrt(ms + EPS) * gain_ref[...]
        gate = mod_ref[0][gate_idx:gate_idx + 1]
        o_ref[...] = x_ref[...] + gate * y


def _down(g, w, x, mod, gain, *, layer, seq, gate_idx):
    t_rows, kdim = g.shape
    d = w.shape[2]
    tm = _tile(seq, 512)
    tk = _lane_tile(kdim, 2816)
    tps = seq // tm
    nk = kdim // tk
    return pl.pallas_call(
        functools.partial(_down_kernel, gate_idx=gate_idx, nk=nk),
        out_shape=jax.ShapeDtypeStruct((t_rows, d), F32),
        grid=(t_rows // tm, nk),
        in_specs=[
            pl.BlockSpec((tm, tk), lambda i, k: (i, k)),
            pl.BlockSpec((None, tk, d), lambda i, k: (layer, k, 0)),
            pl.BlockSpec((tm, d), lambda i, k: (i, 0)),
            pl.BlockSpec((1, 6, d), lambda i, k: (i // tps, 0, 0)),
            pl.BlockSpec((1, d), lambda i, k: (0, 0)),
        ],
        out_specs=pl.BlockSpec((tm, d), lambda i, k: (i, 0)),
        scratch_shapes=[pltpu.VMEM((tm, d), F32)],
        compiler_params=_cparams("arbitrary", "arbitrary"),
        name="down",
    )(g, w, x, mod, gain)


def _pool_kernel(x_ref, xp_ref, xn_ref, mod_ref, gpre_ref, gpost_ref, pw_ref, ps_ref, o_ref, h_scr,
                 *, tm, tps, seq):
    _fill_h(h_scr, x_ref, xp_ref, xn_ref, mod_ref, gpre_ref, 0, 1, tm)
    keep_first, keep_last = _edge_keep(tps)
    rows = tm + 2 * HALO
    d = x_ref.shape[1]
    cg = d // len(POOL_WINDOWS)
    pos = (pl.program_id(0) % tps) * tm + lax.broadcasted_iota(jnp.int32, (tm, 1), 0)
    ys = []
    ssq = jnp.zeros((tm, 1), F32)
    for g, win in enumerate(POOL_WINDOWS):
        half = win // 2
        sl = slice(g * cg, (g + 1) * cg)
        hg = jnp.concatenate([h_scr[0:HALO, sl] * keep_first, h_scr[HALO:HALO + tm, sl],
                              h_scr[HALO + tm:, sl] * keep_last], axis=0)
        p = hg
        s = 1
        while s < win:
            p = p + pltpu.roll(p, s, 0)
            s *= 2
        if half > 1:
            p = pltpu.roll(p, rows - (half - 1), 0)
        lo = jnp.maximum(pos - half, 0)
        hi = jnp.minimum(pos + (half - 1), seq - 1)
        cnt = (hi - lo + 1).astype(F32)
        pooled = p[HALO:HALO + tm] / cnt - hg[HALO:HALO + tm]
        y = jnp.dot(pooled.astype(BF16), pw_ref[g], preferred_element_type=F32) * ps_ref[:, sl]
        ssq = ssq + jnp.sum(y * y, axis=-1, keepdims=True)
        ys.append(y)
    inv = lax.rsqrt(ssq / d + EPS)
    gate = mod_ref[0][2:3]
    for g in range(len(POOL_WINDOWS)):
        sl = slice(g * cg, (g + 1) * cg)
        o_ref[:, sl] = x_ref[:, sl] + gate[:, sl] * (ys[g] * inv * gpost_ref[:, sl])


def _pool_layer(x, mod, gpre, gpost, pw, ps, *, seq):
    t_rows, d = x.shape
    tm = _tile(seq, 256)
    tps = seq // tm
    prev, nxt = _halo_maps(tm, t_rows)
    return pl.pallas_call(
        functools.partial(_pool_kernel, tm=tm, tps=tps, seq=seq),
        out_shape=jax.ShapeDtypeStruct((t_rows, d), F32),
        grid=(t_rows // tm,),
        in_specs=[
            pl.BlockSpec((tm, d), lambda i: (i, 0)),
            pl.BlockSpec((HALO, d), prev),
            pl.BlockSpec((HALO, d), nxt),
            pl.BlockSpec((1, 6, d), lambda i: (i // tps, 0, 0)),
            pl.BlockSpec((1, d), lambda i: (0, 0)),
            pl.BlockSpec((1, d), lambda i: (0, 0)),
            pl.BlockSpec(pw.shape, lambda i: (0, 0, 0)),
            pl.BlockSpec((1, d), lambda i: (0, 0)),
        ],
        out_specs=pl.BlockSpec((tm, d), lambda i: (i, 0)),
        scratch_shapes=[pltpu.VMEM((tm + 2 * HALO, d), F32)],
        compiler_params=_cparams("arbitrary"),
        name="pool_layer",
    )(x, x, x, mod, gpre, gpost, pw, ps)


def _rope_tables(seq):
    pos = jnp.arange(seq, dtype=jnp.int32)
    row = (pos // GRID_W).astype(F32)[:, None]
    col = (pos % GRID_W).astype(F32)[:, None]
    axis_dim = HEAD_DIM // 2
    inv = ROPE_THETA ** (-jnp.arange(0, axis_dim, 2, dtype=F32) / axis_dim)
    ar, ac = row * inv, col * inv
    cos = jnp.concatenate([jnp.cos(ar), jnp.cos(ar), jnp.cos(ac), jnp.cos(ac)], axis=-1)
    sin = jnp.concatenate([-jnp.sin(ar), jnp.sin(ar), -jnp.sin(ac), jnp.sin(ac)], axis=-1)
    return cos, sin


def _qkv_kernel(x_ref, mod_ref, g_ref, w0_ref, w1_ref, hg_ref, cos_ref, sin_ref, q_ref, k_ref, vt_ref, h_scr, *,
                n_q_steps):
    j = pl.program_id(1)
    tn = k_ref.shape[1]

    @pl.when(j == 0)
    def _():
        m = mod_ref[0]
        h_scr[...] = _modnorm(x_ref[...], g_ref[...], 1.0 + m[1:2], m[0:1]).astype(BF16)

    def proj(t):
        return jnp.dot(h_scr[...], (w0_ref, w1_ref)[t][...], preferred_element_type=F32)

    def norm_rope(a, gain, out_scale, o_ref, col0):
        cos = cos_ref[...]
        sin = sin_ref[...]
        quarter = HEAD_DIM // 4
        lane = lax.broadcasted_iota(jnp.int32, (1, HEAD_DIM), 1)
        low = (lane % (2 * quarter)) < quarter
        for h in range(tn // HEAD_DIM):
            v = a[:, h * HEAD_DIM:(h + 1) * HEAD_DIM]
            ms = jnp.mean(v * v, axis=-1, keepdims=True)
            v = v * lax.rsqrt(ms + EPS) * gain
            swapped = jnp.where(low, pltpu.roll(v, HEAD_DIM - quarter, 1), pltpu.roll(v, quarter, 1))
            o_ref[:, col0 + h * HEAD_DIM:col0 + (h + 1) * HEAD_DIM] = (
                (v * cos + swapped * sin) * out_scale).astype(o_ref.dtype)

    @pl.when(j < n_q_steps)
    def _():
        for t in range(2):
            norm_rope(proj(t), hg_ref[0:1], HEAD_DIM ** -0.5 * LOG2E, q_ref, t * tn)

    @pl.when(j == n_q_steps)
    def _():
        norm_rope(proj(0), hg_ref[1:2], 1.0, k_ref, 0)
        v = proj(1)
        for c in range(vt_ref.shape[0]):
            vt_ref[c] = v[c * FLASH_TK:(c + 1) * FLASH_TK].T.astype(vt_ref.dtype)


def _qkv(x, mod, gain, w, head_gains, cos, sin, *, seq, n_kv):
    t_rows, d = x.shape
    n_out = w.shape[1]
    tm = _tile(seq, 512)
    tn = HEAD_DIM * n_kv
    tps = seq // tm
    n_q_steps = d // (2 * tn)
    assert tm % FLASH_TK == 0 and n_out == d + 2 * tn and d % (2 * tn) == 0
    return pl.pallas_call(
        functools.partial(_qkv_kernel, n_q_steps=n_q_steps),
        out_shape=(jax.ShapeDtypeStruct((t_rows, d), BF16), jax.ShapeDtypeStruct((t_rows, tn), BF16),
                   jax.ShapeDtypeStruct((t_rows // FLASH_TK, tn, FLASH_TK), BF16)),
        grid=(t_rows // tm, n_q_steps + 1),
        in_specs=[
            pl.BlockSpec((tm, d), lambda i, j: (i, 0)),
            pl.BlockSpec((1, 6, d), lambda i, j: (i // tps, 0, 0)),
            pl.BlockSpec((1, d), lambda i, j: (0, 0)),
            pl.BlockSpec((d, tn), lambda i, j: (0, 2 * j)),
            pl.BlockSpec((d, tn), lambda i, j: (0, 2 * j + 1)),
            pl.BlockSpec((2, HEAD_DIM), lambda i, j: (0, 0)),
            pl.BlockSpec((tm, HEAD_DIM), lambda i, j: (i % tps, 0)),
            pl.BlockSpec((tm, HEAD_DIM), lambda i, j: (i % tps, 0)),
        ],
        out_specs=(
            pl.BlockSpec((tm, 2 * tn), lambda i, j: (i, jnp.minimum(j, n_q_steps - 1))),
            pl.BlockSpec((tm, tn), lambda i, j: (i, 0)),
            pl.BlockSpec((tm // FLASH_TK, tn, FLASH_TK), lambda i, j: (i, 0, 0)),
        ),
        scratch_shapes=[pltpu.VMEM((tm, d), BF16)],
        compiler_params=_cparams("arbitrary", "arbitrary"),
        name="qkv_rope",
    )(x, mod, gain, w, w, head_gains, cos, sin)


def _flash_kernel(q_ref, k_ref, vt_ref, o_ref, acc_ref, s_ref, *, tq, nk):
    q = q_ref[...]
    qs = jnp.concatenate([q[:, h * HEAD_DIM:(h + 1) * HEAD_DIM] for h in range(GQA_GROUP)], axis=0)
    ones = jnp.ones((ONES_ROWS, FLASH_TK), BF16)
    acc_ref[...] = jnp.zeros_like(acc_ref)

    def scores(kk, slot):
        start = pl.multiple_of(kk * FLASH_TK, FLASH_TK)
        s_ref[slot] = lax.dot_general(k_ref[pl.ds(start, FLASH_TK), :], qs, (((1,), (1,)), ((), ())),
                                      preferred_element_type=F32)

    def accumulate(kk, slot, m):
        st = s_ref[slot]
        m_new = jnp.maximum(m, jnp.max(st, axis=0, keepdims=True))
        alpha = jnp.exp2(m - m_new)
        pt = jnp.exp2(st - m_new).astype(BF16)
        va = jnp.concatenate([vt_ref[kk], ones], axis=0)
        acc_ref[...] = alpha * acc_ref[...] + jnp.dot(va, pt, preferred_element_type=F32)
        return m_new

    scores(0, 0)
    unroll = 4 if nk % 4 == 0 else 2

    def group(i, m):
        k0 = unroll * i
        for u in range(unroll):
            nxt = k0 + u + 1
            scores(nxt if u + 1 < unroll else jnp.minimum(nxt, nk - 1), (u + 1) % 2)
            m = accumulate(k0 + u, u % 2, m)
        return m

    lax.fori_loop(0, nk // unroll, group, jnp.full((1, GQA_GROUP * tq), -jnp.inf, F32))
    acc = acc_ref[...]
    ot = acc[:HEAD_DIM] / acc[HEAD_DIM:HEAD_DIM + 1]
    for h in range(GQA_GROUP):
        o_ref[:, h * HEAD_DIM:(h + 1) * HEAD_DIM] = ot[:, h * tq:(h + 1) * tq].T.astype(o_ref.dtype)


def _flash(q, k, vt, *, batch, seq, n_kv):
    t_rows, d = q.shape
    tq = _tile(seq, 512)
    nqb = seq // tq
    nk = seq // FLASH_TK
    assert nk % 2 == 0
    gw = GQA_GROUP * HEAD_DIM
    return pl.pallas_call(
        functools.partial(_flash_kernel, tq=tq, nk=nk),
        out_shape=jax.ShapeDtypeStruct((t_rows, d), BF16),
        grid=(batch, n_kv, nqb),
        in_specs=[
            pl.BlockSpec((tq, gw), lambda b, kv, qi: (b * nqb + qi, kv)),
            pl.BlockSpec((seq, HEAD_DIM), lambda b, kv, qi: (b, kv)),
            pl.BlockSpec((nk, HEAD_DIM, FLASH_TK), lambda b, kv, qi: (b, kv, 0)),
        ],
        out_specs=pl.BlockSpec((tq, gw), lambda b, kv, qi: (b * nqb + qi, kv)),
        scratch_shapes=[pltpu.VMEM((HEAD_DIM + ONES_ROWS, GQA_GROUP * tq), F32),
                        pltpu.VMEM((2, FLASH_TK, GQA_GROUP * tq), F32)],
        compiler_params=_cparams("arbitrary", "arbitrary", "arbitrary"),
        name="flash_attention",
    )(q, k, vt)


def _dft_sizes(seq):
    n = 2 * seq
    n2 = DFT_N2_MAX
    while n // n2 < DFT_N1_MIN:
        n2 //= 2
    return n, n // n2, n2


def _dft_tables(seq):
    n, n1, n2 = _dft_sizes(seq)
    h2 = n2 // 2
    k2 = jnp.arange(h2, dtype=jnp.int32)
    m2 = jnp.arange(h2, dtype=jnp.int32)
    idx = (m2[None, :] * (2 * k2[:, None] + 1)) % (2 * n2)
    th = idx.astype(F32) * (2.0 * math.pi / (2 * n2))
    m1 = jnp.concatenate([jnp.cos(th), -jnp.sin(th)], axis=0)
    a1 = jnp.arange(n1, dtype=jnp.int32)
    k1 = jnp.arange(n1, dtype=jnp.int32)
    freq = 2 * n2 * k1[None, :, None] + 2 * k2[:, None, None] + 1
    idx = (a1[None, None, :] * freq) % (2 * n)
    ph = idx.astype(F32) * (2.0 * math.pi / (2 * n))
    c, s = jnp.cos(ph), jnp.sin(ph)
    gb = jnp.concatenate([jnp.concatenate([c, s], axis=2), jnp.concatenate([-s, c], axis=2)], axis=1)
    return m1, gb


def _filt_kernel(z_ref, w1_ref, b1_ref, w2_ref, b2_ref, fr_ref, w3_ref, dl_ref, o_ref):
    z = z_ref[...]
    fr = fr_ref[...]
    a = jnp.sin(fr * (_dot3(z, w1_ref[...]) + b1_ref[...]))
    a = jnp.sin(fr * (_dot3(a, w2_ref[...]) + b2_ref[...]))
    decay = jnp.exp(-z[:, 0:1] * dl_ref[...]) + HYENA_MOD_SHIFT
    d = decay.shape[1]
    row = pl.program_id(0) * z.shape[0] + lax.broadcasted_iota(jnp.int32, (z.shape[0], 1), 0)
    for c in range(w3_ref.shape[1] // d):
        f = _dot3(a, w3_ref[:, c * d:(c + 1) * d]) * decay
        o_ref[:, c * d:(c + 1) * d] = jnp.where(row == 0, 0.0, f) if c % 2 == 1 else f


def _filters(z, w1, b1, w2, b2, fr, w3, deltas):
    seq = z.shape[0]
    d = deltas.shape[1]
    tl = _tile(seq, 256)
    hid = w2.shape[0]
    const = lambda i: (0, 0)
    return pl.pallas_call(
        _filt_kernel,
        out_shape=jax.ShapeDtypeStruct((seq, w3.shape[1]), F32),
        grid=(seq // tl,),
        in_specs=[
            pl.BlockSpec((tl, HYENA_EMB_PAD), lambda i: (i, 0)),
            pl.BlockSpec((HYENA_EMB_PAD, hid), const),
            pl.BlockSpec((1, hid), const),
            pl.BlockSpec((hid, hid), const),
            pl.BlockSpec((1, hid), const),
            pl.BlockSpec((1, hid), const),
            pl.BlockSpec(w3.shape, const),
            pl.BlockSpec((1, d), const),
        ],
        out_specs=pl.BlockSpec((tl, w3.shape[1]), lambda i: (i, 0)),
        compiler_params=_cparams("arbitrary"),
        name="hyena_filters",
    )(z, w1, b1, w2, b2, fr, w3, deltas)


def _dft1_kernel(x_ref, m_ref, o_ref, y_scr):
    xt = pltpu.einshape("hjd->jhd", x_ref[0])
    m = m_ref[...]
    for j in range(DFT_ROWS):
        y_scr[j] = _dft_dot(m, xt[j])
    yt = pltpu.einshape("jnd->njd", y_scr[...]).astype(o_ref.dtype)
    h2 = yt.shape[0] // 2
    o_ref[0, 0] = yt[:h2]
    o_ref[0, 1] = yt[h2:]


def _dft1(x4, m1b, *, td, n_blocks, col_map):
    b, h2, n1, _ = x4.shape
    n2 = m1b.shape[0]
    return pl.pallas_call(
        _dft1_kernel,
        out_shape=jax.ShapeDtypeStruct((b, 2, h2, n1, n_blocks * td), BF16),
        grid=(b, n1 // DFT_ROWS, n_blocks),
        in_specs=[
            pl.BlockSpec((1, h2, DFT_ROWS, td), lambda bi, g, c: (bi, 0, g, col_map(c))),
            pl.BlockSpec(m1b.shape, lambda bi, g, c: (0, 0)),
        ],
        out_specs=pl.BlockSpec((1, 2, h2, DFT_ROWS, td), lambda bi, g, c: (bi, 0, 0, g, c)),
        scratch_shapes=[pltpu.VMEM((DFT_ROWS, n2, td), F32)],
        compiler_params=_cparams("arbitrary", "arbitrary", "arbitrary"),
        name="dft_stage1",
    )(x4, m1b)


def _fspec_kernel(af_ref, ab_ref, g_ref, o_ref, *, kb, n1):
    xf = [_dft_dot(g_ref[kk], jnp.concatenate([af_ref[0, kk], af_ref[1, kk]], axis=0)) for kk in range(kb)]
    xb = [_dft_dot(g_ref[kk], jnp.concatenate([ab_ref[0, kk], ab_ref[1, kk]], axis=0)) for kk in range(kb)]
    for kk in range(kb):
        o_ref[0, 0, kk] = xf[kk][:n1] + xb[kk][:n1]
        o_ref[0, 1, kk] = xf[kk][n1:] - xb[kk][n1:]


def _filter_spectrum(af, gbb, *, d, kb, td):
    _, h2, n1, w = af.shape
    ndt = d // td
    norder = w // (2 * d)
    return pl.pallas_call(
        functools.partial(_fspec_kernel, kb=kb, n1=n1),
        out_shape=jax.ShapeDtypeStruct((norder, 2, h2, n1, d), F32),
        grid=(norder, h2 // kb, ndt),
        in_specs=[
            pl.BlockSpec((2, kb, n1, td), lambda o, k, t: (0, k, 0, (2 * o) * ndt + t)),
            pl.BlockSpec((2, kb, n1, td), lambda o, k, t: (0, k, 0, (2 * o + 1) * ndt + t)),
            pl.BlockSpec((kb, 2 * n1, 2 * n1), lambda o, k, t: (k, 0, 0)),
        ],
        out_specs=pl.BlockSpec((1, 2, kb, n1, td), lambda o, k, t: (o, 0, k, 0, t)),
        compiler_params=_cparams("arbitrary", "arbitrary", "arbitrary"),
        name="hyena_filter_spectrum",
    )(af, af, gbb)


def _cmid_kernel(a_ref, ks_ref, g_ref, gt_ref, o_ref, *, kb, n1):
    xs = [_dft_dot(g_ref[kk], jnp.concatenate([a_ref[0, 0, kk], a_ref[0, 1, kk]], axis=0)) for kk in range(kb)]
    ys = []
    for kk in range(kb):
        xr, xi = xs[kk][:n1], xs[kk][n1:]
        kr, ki = ks_ref[0, 0, kk], ks_ref[0, 1, kk]
        ys.append(jnp.concatenate([xr * kr - xi * ki, xr * ki + xi * kr], axis=0))
    bps = [_dft_dot(gt_ref[kk], ys[kk]) for kk in range(kb)]
    for kk in range(kb):
        o_ref[0, 0, kk] = bps[kk][:n1].astype(o_ref.dtype)
        o_ref[0, 1, kk] = bps[kk][n1:].astype(o_ref.dtype)


def _conv_mid(a, kspec, order, gbb, gtb, *, kb, td):
    b, _, h2, n1, d = a.shape
    return pl.pallas_call(
        functools.partial(_cmid_kernel, kb=kb, n1=n1),
        out_shape=jax.ShapeDtypeStruct(a.shape, F32),
        grid=(h2 // kb, d // td, b),
        in_specs=[
            pl.BlockSpec((1, 2, kb, n1, td), lambda k, t, bi: (bi, 0, k, 0, t)),
            pl.BlockSpec((1, 2, kb, n1, td), lambda k, t, bi: (order, 0, k, 0, t)),
            pl.BlockSpec((kb, 2 * n1, 2 * n1), lambda k, t, bi: (k, 0, 0)),
            pl.BlockSpec((kb, 2 * n1, 2 * n1), lambda k, t, bi: (k, 0, 0)),
        ],
        out_specs=pl.BlockSpec((1, 2, kb, n1, td), lambda k, t, bi: (bi, 0, k, 0, t)),
        compiler_params=_cparams("arbitrary", "arbitrary", "arbitrary"),
        name="hyena_conv_mid",
    )(a, kspec, gbb, gtb)


def _idft1_kernel(bp_ref, m_ref, gate_ref, zz_ref, skip_ref, o_ref, y_scr):
    bp = jnp.concatenate([bp_ref[0, 0], bp_ref[0, 1]], axis=0)
    bt = pltpu.einshape("njd->jnd", bp.astype(F32))
    m = m_ref[...]
    for j in range(IDFT_ROWS):
        y_scr[j] = _dft_dot(m, bt[j])
    y = pltpu.einshape("jhd->hjd", y_scr[...])
    o_ref[0] = (gate_ref[0] * (y + zz_ref[0] * skip_ref[...])).astype(o_ref.dtype)


def _idft1_gate(bp, minvb, u4, zz4, skip, *, td, gate_part, zz_part):
    b, _, h2, n1, d = bp.shape
    nt = d // td
    return pl.pallas_call(
        _idft1_kernel,
        out_shape=jax.ShapeDtypeStruct((b, h2, n1, d), F32),
        grid=(b, n1 // IDFT_ROWS, nt),
        in_specs=[
            pl.BlockSpec((1, 2, h2, IDFT_ROWS, td), lambda bi, g, c: (bi, 0, 0, g, c)),
            pl.BlockSpec(minvb.shape, lambda bi, g, c: (0, 0)),
            pl.BlockSpec((1, h2, IDFT_ROWS, td), lambda bi, g, c: (bi, 0, g, gate_part * nt + c)),
            pl.BlockSpec((1, h2, IDFT_ROWS, td), lambda bi, g, c: (bi, 0, g, zz_part * nt + c)),
            pl.BlockSpec((1, td), lambda bi, g, c: (0, c)),
        ],
        out_specs=pl.BlockSpec((1, h2, IDFT_ROWS, td), lambda bi, g, c: (bi, 0, g, c)),
        scratch_shapes=[pltpu.VMEM((IDFT_ROWS, h2, td), F32)],
        compiler_params=_cparams("arbitrary", "arbitrary", "arbitrary"),
        name="idft_stage1_gate",
    )(bp, minvb, u4, zz4, skip)


def _hyena_features(seq, d):
    t = jnp.linspace(0.0, 1.0, seq, dtype=F32)[:, None]
    w = 2.0 * math.pi * jnp.arange(seq, dtype=F32)[:, None] / seq
    f = jnp.linspace(1e-4, HYENA_BANDS - 1, HYENA_BANDS, dtype=F32)[None, :]
    z = jnp.concatenate([t, jnp.cos(f * w), -jnp.sin(f * w)], axis=-1)
    z = jnp.pad(z, ((0, 0), (0, HYENA_EMB_PAD - HYENA_EMB)))
    deltas = jnp.abs(jnp.linspace(HYENA_MIN_DECAY, HYENA_MAX_DECAY, d, dtype=F32))[None, :]
    return z, deltas


def _hyena_mixer(x, mod, gain, p, *, batch, seq):
    t_rows, d = x.shape
    n, n1, n2 = _dft_sizes(seq)
    h2 = n2 // 2
    u = _up(x, mod, gain, p["w_in"][None], p["b_in"], p["conv_w"], p["conv_b"], layer=0, seq=seq, sh_idx=0,
            sc_idx=1, glu=False, out_dtype=F32)

    z, deltas = _hyena_features(seq, d)
    filt = _filters(z, p["f_w1"], p["f_b1"], p["f_w2"], p["f_b2"], p["f_freq"], p["f_w3"], deltas)

    m1, gb = _dft_tables(seq)
    m1b = m1.astype(BF16)
    minvb = (m1.T * (2.0 / n)).astype(BF16)
    gbb = gb.astype(BF16)
    gtb = jnp.swapaxes(gb, 1, 2).astype(BF16)
    kb = max(1, 256 // n1)
    td = _tile(d, 1024)
    ts = _tile(d, 256)
    nts = d // ts

    ncf = filt.shape[1]
    af = _dft1(filt.reshape(1, h2, n1, ncf), m1b, td=ts, n_blocks=ncf // ts, col_map=lambda c: c)
    kspec = _filter_spectrum(af[0], gbb, d=d, kb=kb, td=td)

    u4 = u.reshape(batch, h2, n1, 3 * d)
    zz4 = u4
    for o in range(HYENA_ORDER):
        a = _dft1(zz4, m1b, td=ts, n_blocks=nts, col_map=lambda c: c)
        bp = _conv_mid(a, kspec, o, gbb, gtb, kb=kb, td=td)
        zz4 = _idft1_gate(bp, minvb, u4, zz4, p["skip"][o:o + 1], td=ts, gate_part=1 + o, zz_part=0)
    return zz4.reshape(t_rows, d)


def _trunk(x3, modall, row0, p):
    batch, seq, d = x3.shape
    t_rows = batch * seq
    x = x3.reshape(t_rows, d)
    n_kv = d // HEAD_DIM // GQA_GROUP
    depth = modall.shape[0]
    for i in range(depth):
        mod = modall[i, row0:row0 + batch].reshape(batch, 6, d)
        kind, j = i % 3, i // 3
        g_pre, g_post = p["norm_mix_pre"][i:i + 1], p["norm_mix_post"][i:i + 1]
        if kind == 0:
            x = _pool_layer(x, mod, g_pre, g_post, p["pool_w"][j], p["pool_scale"][j:j + 1], seq=seq)
        elif kind == 1:
            hp = {k: v[j] for k, v in p["hy"].items()}
            zz = _hyena_mixer(x, mod, g_pre, hp, batch=batch, seq=seq)
            x = _down(zz, p["hy"]["w_out"], x, mod, g_post, layer=j, seq=seq, gate_idx=2)
        else:
            cos, sin = _rope_tables(seq)
            gains = jnp.stack([p["at_q_gain"][j], p["at_k_gain"][j]], axis=0)
            q, k, vt = _qkv(x, mod, g_pre, p["at_w_qkv"][j], gains, cos, sin, seq=seq, n_kv=n_kv)
            o = _flash(q, k, vt, batch=batch, seq=seq, n_kv=n_kv)
            x = _down(o, p["at_w_o"], x, mod, g_post, layer=j, seq=seq, gate_idx=2)
        g = _up(x, mod, p["norm_ffn_pre"][i:i + 1], p["ffn_w_up"], None, p["ffn_conv_w"][i],
                p["ffn_conv_b"][i:i + 1], layer=i, seq=seq, sh_idx=3, sc_idx=4, glu=True, out_dtype=BF16)
        x = _down(g, p["ffn_w_down"], x, mod, p["norm_ffn_post"][i:i + 1], layer=i, seq=seq, gate_idx=5)
    return x.reshape(batch, seq, d)


def kernel(x_prompt, x_sample, c_prompt, c_sample, mod_w, mod_b, norm_mix_pre, norm_mix_post, norm_ffn_pre,
           norm_ffn_post, ffn_w_up, ffn_conv_w, ffn_conv_b, ffn_w_down, pool_w, pool_scale, hy_w_in, hy_b_in,
           hy_conv_w, hy_conv_b, hy_f_w1, hy_f_b1, hy_f_w2, hy_f_b2, hy_f_w3, hy_f_freq, hy_skip, hy_w_out,
           at_w_qkv, at_q_gain, at_k_gain, at_w_o):
    nb = c_prompt.shape[0]
    ns = c_sample.shape[0]
    assert nb + ns <= MOD_ROWS
    c8 = jnp.concatenate([c_prompt, c_sample, jnp.zeros((MOD_ROWS - nb - ns, c_prompt.shape[1]), F32)], axis=0)
    modall = _modulation(c8, mod_w, mod_b)
    row = lambda v: v[:, None, :]
    p = dict(
        norm_mix_pre=norm_mix_pre, norm_mix_post=norm_mix_post, norm_ffn_pre=norm_ffn_pre,
        norm_ffn_post=norm_ffn_post,
        ffn_w_up=ffn_w_up.astype(BF16), ffn_conv_w=ffn_conv_w, ffn_conv_b=ffn_conv_b,
        ffn_w_down=ffn_w_down.astype(BF16),
        pool_w=pool_w.astype(BF16), pool_scale=pool_scale,
        hy=dict(
            w_in=hy_w_in.astype(BF16), b_in=row(hy_b_in), conv_w=hy_conv_w, conv_b=row(hy_conv_b),
            f_w1=jnp.pad(hy_f_w1, ((0, 0), (0, HYENA_EMB_PAD - HYENA_EMB), (0, 0))), f_b1=row(hy_f_b1),
            f_w2=hy_f_w2, f_b2=row(hy_f_b2), f_w3=hy_f_w3, f_freq=row(hy_f_freq), skip=hy_skip,
            w_out=hy_w_out.astype(BF16)),
        at_w_qkv=at_w_qkv.astype(BF16), at_q_gain=at_q_gain, at_k_gain=at_k_gain, at_w_o=at_w_o.astype(BF16),
    )
    y_prompt = _trunk(x_prompt, modall, 0, p)
    y_sample = _trunk(x_sample, modall, nb, p)
    return (y_prompt, y_sample)
```

```python
import functools
import math

import jax
import jax.numpy as jnp
from jax import lax
from jax.experimental import pallas as pl
from jax.experimental.pallas import tpu as pltpu

EPS = 1e-6
HEAD_DIM = 128
GQA_GROUP = 4
GRID_W = 64
ROPE_THETA = 10000.0
POOL_WINDOWS = (2, 4, 8, 16)
HYENA_ORDER = 2
HYENA_EMB = 33
HYENA_BANDS = (HYENA_EMB - 1) // 2
HYENA_EMB_PAD = 64
HYENA_FAST_DECAY = 0.3
HYENA_SLOW_DECAY = 1.5
HYENA_TARGET = 1e-2
HYENA_MOD_SHIFT = 0.05
HYENA_MIN_DECAY = math.log(HYENA_TARGET) / HYENA_SLOW_DECAY
HYENA_MAX_DECAY = math.log(HYENA_TARGET) / HYENA_FAST_DECAY

HALO = 16
MOD_ROWS = 8
DFT_N2_MAX = 512
DFT_N1_MIN = 16
DFT_ROWS = 16
IDFT_ROWS = 8
LANES = 128
UP_SUB = 512
FLASH_TK = 512
ONES_ROWS = 16
LOG2E = 1.4426950408889634
VMEM_LIMIT = 58 * 1024 * 1024

UP_TM = 1024
DOWN_TM = 512
DOWN_TK_MAX = 2816
POOL_TM = 256
QKV_TM = 512
FLASH_TQ = 512
MOD_TN = 1024
FILT_ROWS = 256
MID_LANES = 1024
DFT_LANES = 256

F32 = jnp.float32
BF16 = jnp.bfloat16


def _cparams(*sem):
    return pltpu.CompilerParams(dimension_semantics=sem, vmem_limit_bytes=VMEM_LIMIT)


def _tile(dim, pref):
    t = min(dim, pref)
    while dim % t:
        t //= 2
    return t


def _lane_tile(dim, cap):
    units = dim // LANES
    best = max(u for u in range(1, units + 1) if units % u == 0 and u * LANES <= max(cap, LANES))
    return best * LANES


def _split3(x):
    hi = x.astype(BF16)
    lo = (x - hi.astype(F32)).astype(BF16)
    return hi, lo


def _dot3(a, b):
    ah, al = _split3(a)
    bh, bl = _split3(b)
    d = functools.partial(jnp.dot, preferred_element_type=F32)
    return d(ah, bh) + (d(ah, bl) + d(al, bh))


def _dft_dot(m, x):
    return jnp.dot(m, x.astype(BF16), preferred_element_type=F32)


def _modnorm(xv, gain, scale1p, shift):
    ms = jnp.mean(xv * xv, axis=-1, keepdims=True)
    return xv * lax.rsqrt(ms + EPS) * gain * scale1p + shift


def _halo_maps(tm, t_rows):
    r = tm // HALO
    last = t_rows // HALO - 1
    prev = lambda i, *_: (jnp.maximum(i * r - 1, 0), 0)
    nxt = lambda i, *_: (jnp.minimum((i + 1) * r, last), 0)
    return prev, nxt


def _mod_kernel(c_ref, w_ref, b_ref, o_ref):
    c = c_ref[...]
    a = (c / (1.0 + jnp.exp(-c))).astype(BF16)
    o_ref[0] = jnp.dot(a, w_ref[0].astype(BF16), preferred_element_type=F32) + b_ref[0]


def _modulation(c8, mod_w, mod_b):
    depth, d, n = mod_w.shape
    tn = _tile(n, MOD_TN)
    return pl.pallas_call(
        _mod_kernel,
        out_shape=jax.ShapeDtypeStruct((depth, MOD_ROWS, n), F32),
        grid=(depth, n // tn),
        in_specs=[
            pl.BlockSpec((MOD_ROWS, d), lambda l, j: (0, 0)),
            pl.BlockSpec((1, d, tn), lambda l, j: (l, 0, j)),
            pl.BlockSpec((1, 1, tn), lambda l, j: (l, 0, j)),
        ],
        out_specs=pl.BlockSpec((1, MOD_ROWS, tn), lambda l, j: (l, 0, j)),
        compiler_params=_cparams("arbitrary", "arbitrary"),
        name="modulation",
    )(c8, mod_w, mod_b.reshape(depth, 1, n))


def _fill_h(h_scr, x_ref, xp_ref, xn_ref, mod_ref, g_ref, sh_idx, sc_idx, tm):
    m = mod_ref[0]
    shift = m[sh_idx:sh_idx + 1]
    scale1p = 1.0 + m[sc_idx:sc_idx + 1]
    gain = g_ref[...]
    h_scr[0:HALO] = _modnorm(xp_ref[...], gain, scale1p, shift).astype(h_scr.dtype)
    h_scr[HALO:HALO + tm] = _modnorm(x_ref[...], gain, scale1p, shift).astype(h_scr.dtype)
    h_scr[HALO + tm:] = _modnorm(xn_ref[...], gain, scale1p, shift).astype(h_scr.dtype)


def _edge_keep(tps):
    i = pl.program_id(0) % tps
    return (i != 0).astype(F32), (i != tps - 1).astype(F32)


def _conv3_rows(a, keep_first, keep_last, cw, cb, tm):
    rows = tm + 2 * HALO
    a = jnp.concatenate([a[0:HALO] * keep_first, a[HALO:HALO + tm], a[HALO + tm:] * keep_last], axis=0)
    prev = pltpu.roll(a, 1, 0)
    nxt = pltpu.roll(a, rows - 1, 0)
    c = prev * cw[0:1] + a * cw[1:2] + nxt * cw[2:3] + cb
    return c[HALO:HALO + tm]


def _gelu_tanh(x):
    return 0.5 * x * (1.0 + jnp.tanh(0.7978845608028654 * (x + 0.044715 * (x * x * x))))


def _up_kernel(*refs, sh_idx, sc_idx, tm, tps, glu, has_bias, n_chunks):
    x_ref, xp_ref, xn_ref, mod_ref, g_ref, w_ref = refs[:6]
    k = 6
    wb_ref = b_ref = None
    if glu:
        wb_ref = refs[k]
        k += 1
    if has_bias:
        b_ref = refs[k]
        k += 1
    cw_ref, cb_ref, o_ref, h_scr, hn_scr = refs[k:k + 5]
    i = pl.program_id(0)
    j = pl.program_id(1)

    @pl.when(jnp.logical_and(j == 0, i == 0))
    def _():
        _fill_h(h_scr, x_ref, xp_ref, xn_ref, mod_ref, g_ref, sh_idx, sc_idx, tm)

    @pl.when(jnp.logical_and(j == 0, i > 0))
    def _():
        h_scr[...] = hn_scr[...]

    m = mod_ref[0]
    shift = m[sh_idx:sh_idx + 1]
    scale1p = 1.0 + m[sc_idx:sc_idx + 1]
    gain = g_ref[...]

    @pl.when(j == 1)
    def _():
        hn_scr[0:HALO] = _modnorm(xp_ref[...], gain, scale1p, shift).astype(hn_scr.dtype)
        hn_scr[HALO + tm:] = _modnorm(xn_ref[...], gain, scale1p, shift).astype(hn_scr.dtype)

    rc = tm // n_chunks
    chunk = jnp.clip(j - 1, 0, n_chunks - 1)
    r0 = pl.multiple_of(chunk * rc, HALO)
    hn_scr[pl.ds(pl.multiple_of(HALO + r0, HALO), rc), :] = _modnorm(
        x_ref[pl.ds(r0, rc), :], gain, scale1p, shift).astype(hn_scr.dtype)

    keep_first, keep_last = _edge_keep(tps)
    for s in range(o_ref.shape[1] // UP_SUB):
        sl = slice(s * UP_SUB, (s + 1) * UP_SUB)
        a = jnp.dot(h_scr[...], w_ref[:, sl], preferred_element_type=F32)
        if has_bias:
            a = a + b_ref[:, sl]
        c = _conv3_rows(a, keep_first, keep_last, cw_ref[:, sl], cb_ref[:, sl], tm)
        if glu:
            b = jnp.dot(h_scr[HALO:HALO + tm], wb_ref[:, sl], preferred_element_type=F32)
            c = _gelu_tanh(c) * b
        o_ref[:, sl] = c.astype(o_ref.dtype)


def _up(x, mod, gain, w, bias, cw, cb, *, layer, seq, sh_idx, sc_idx, glu, out_dtype):
    t_rows, d = x.shape
    n_out = cw.shape[-1]
    tm = _tile(seq, UP_TM)
    tn = _tile(n_out, UP_SUB if glu else 2 * UP_SUB)
    assert tn % UP_SUB == 0
    tps = seq // tm
    nj = n_out // tn
    n_tiles = t_rows // tm
    assert nj >= 2
    n_chunks = 1
    while 2 * n_chunks <= nj - 1 and tm % (2 * n_chunks * HALO) == 0:
        n_chunks *= 2
    r = tm // HALO

    def ahead(i, j):
        return jnp.minimum(i + jnp.where(jnp.logical_and(i == 0, j == 0), 0, 1), n_tiles - 1)

    in_specs = [
        pl.BlockSpec((tm, d), lambda i, j: (ahead(i, j), 0)),
        pl.BlockSpec((HALO, d), lambda i, j: (jnp.maximum(ahead(i, j) * r - 1, 0), 0)),
        pl.BlockSpec((HALO, d), lambda i, j: (jnp.minimum((ahead(i, j) + 1) * r, t_rows // HALO - 1), 0)),
        pl.BlockSpec((1, 6, d), lambda i, j: (ahead(i, j) // tps, 0, 0)),
        pl.BlockSpec((1, d), lambda i, j: (0, 0)),
        pl.BlockSpec((None, d, tn), lambda i, j: (layer, 0, j)),
    ]
    args = [x, x, x, mod, gain, w]
    if glu:
        in_specs.append(pl.BlockSpec((None, d, tn), lambda i, j: (layer, 0, j + nj)))
        args.append(w)
    if bias is not None:
        in_specs.append(pl.BlockSpec((1, tn), lambda i, j: (0, j)))
        args.append(bias)
    in_specs += [pl.BlockSpec((3, tn), lambda i, j: (0, j)), pl.BlockSpec((1, tn), lambda i, j: (0, j))]
    args += [cw, cb]
    return pl.pallas_call(
        functools.partial(_up_kernel, sh_idx=sh_idx, sc_idx=sc_idx, tm=tm, tps=tps, glu=glu,
                          has_bias=bias is not None, n_chunks=n_chunks),
        out_shape=jax.ShapeDtypeStruct((t_rows, n_out), out_dtype),
        grid=(n_tiles, nj),
        in_specs=in_specs,
        out_specs=pl.BlockSpec((tm, tn), lambda i, j: (i, j)),
        scratch_shapes=[pltpu.VMEM((tm + 2 * HALO, d), BF16), pltpu.VMEM((tm + 2 * HALO, d), BF16)],
        compiler_params=_cparams("arbitrary", "arbitrary"),
        name="up_glu" if glu else "up_conv",
    )(*args)


def _down_kernel(g_ref, w_ref, x_ref, mod_ref, gain_ref, o_ref, acc_ref, *, gate_idx, nk):
    k = pl.program_id(1)
    part = jnp.dot(g_ref[...].astype(w_ref.dtype), w_ref[...], preferred_element_type=F32)

    if nk > 1:
        @pl.when(k == 0)
        def _():
            acc_ref[...] = part

    if nk > 2:
        @pl.when(jnp.logical_and(k > 0, k < nk - 1))
        def _():
            acc_ref[...] += part

    @pl.when(k == nk - 1)
    def _():
        f = acc_ref[...] + part if nk > 1 else part
        ms = jnp.mean(f * f, axis=-1, keepdims=True)
        y = f * lax.rsqrt(ms + EPS) * gain_ref[...]
        gate = mod_ref[0][gate_idx:gate_idx + 1]
        o_ref[...] = x_ref[...] + gate * y


def _down(g, w, x, mod, gain, *, layer, seq, gate_idx):
    t_rows, kdim = g.shape
    d = w.shape[2]
    tm = _tile(seq, DOWN_TM)
    tk = _lane_tile(kdim, DOWN_TK_MAX)
    tps = seq // tm
    nk = kdim // tk
    return pl.pallas_call(
        functools.partial(_down_kernel, gate_idx=gate_idx, nk=nk),
        out_shape=jax.ShapeDtypeStruct((t_rows, d), F32),
        grid=(t_rows // tm, nk),
        in_specs=[
            pl.BlockSpec((tm, tk), lambda i, k: (i, k)),
            pl.BlockSpec((None, tk, d), lambda i, k: (layer, k, 0)),
            pl.BlockSpec((tm, d), lambda i, k: (i, 0)),
            pl.BlockSpec((1, 6, d), lambda i, k: (i // tps, 0, 0)),
            pl.BlockSpec((1, d), lambda i, k: (0, 0)),
        ],
        out_specs=pl.BlockSpec((tm, d), lambda i, k: (i, 0)),
        scratch_shapes=[pltpu.VMEM((tm, d), F32)],
        compiler_params=_cparams("arbitrary", "arbitrary"),
        name="down",
    )(g, w, x, mod, gain)


def _pool_kernel(x_ref, xp_ref, xn_ref, mod_ref, gpre_ref, gpost_ref, pw_ref, ps_ref, o_ref, h_scr,
                 *, tm, tps, seq):
    _fill_h(h_scr, x_ref, xp_ref, xn_ref, mod_ref, gpre_ref, 0, 1, tm)
    keep_first, keep_last = _edge_keep(tps)
    rows = tm + 2 * HALO
    d = x_ref.shape[1]
    cg = d // len(POOL_WINDOWS)
    pos = (pl.program_id(0) % tps) * tm + lax.broadcasted_iota(jnp.int32, (tm, 1), 0)
    ys = []
    ssq = jnp.zeros((tm, 1), F32)
    for g, win in enumerate(POOL_WINDOWS):
        half = win // 2
        sl = slice(g * cg, (g + 1) * cg)
        hg = jnp.concatenate([h_scr[0:HALO, sl] * keep_first, h_scr[HALO:HALO + tm, sl],
                              h_scr[HALO + tm:, sl] * keep_last], axis=0)
        p = hg
        s = 1
        while s < win:
            p = p + pltpu.roll(p, s, 0)
            s *= 2
        if half > 1:
            p = pltpu.roll(p, rows - (half - 1), 0)
        lo = jnp.maximum(pos - half, 0)
        hi = jnp.minimum(pos + (half - 1), seq - 1)
        inv_cnt = 1.0 / (hi - lo + 1).astype(F32)
        pooled = p[HALO:HALO + tm] * inv_cnt - hg[HALO:HALO + tm]
        y = jnp.dot(pooled.astype(BF16), pw_ref[g], preferred_element_type=F32) * ps_ref[:, sl]
        ssq = ssq + jnp.sum(y * y, axis=-1, keepdims=True)
        ys.append(y)
    inv = lax.rsqrt(ssq / d + EPS)
    gate = mod_ref[0][2:3]
    for g in range(len(POOL_WINDOWS)):
        sl = slice(g * cg, (g + 1) * cg)
        o_ref[:, sl] = x_ref[:, sl] + gate[:, sl] * (ys[g] * inv * gpost_ref[:, sl])


def _pool_layer(x, mod, gpre, gpost, pw, ps, *, seq):
    t_rows, d = x.shape
    tm = _tile(seq, POOL_TM)
    tps = seq // tm
    prev, nxt = _halo_maps(tm, t_rows)
    return pl.pallas_call(
        functools.partial(_pool_kernel, tm=tm, tps=tps, seq=seq),
        out_shape=jax.ShapeDtypeStruct((t_rows, d), F32),
        grid=(t_rows // tm,),
        in_specs=[
            pl.BlockSpec((tm, d), lambda i: (i, 0)),
            pl.BlockSpec((HALO, d), prev),
            pl.BlockSpec((HALO, d), nxt),
            pl.BlockSpec((1, 6, d), lambda i: (i // tps, 0, 0)),
            pl.BlockSpec((1, d), lambda i: (0, 0)),
            pl.BlockSpec((1, d), lambda i: (0, 0)),
            pl.BlockSpec(pw.shape, lambda i: (0, 0, 0)),
            pl.BlockSpec((1, d), lambda i: (0, 0)),
        ],
        out_specs=pl.BlockSpec((tm, d), lambda i: (i, 0)),
        scratch_shapes=[pltpu.VMEM((tm + 2 * HALO, d), F32)],
        compiler_params=_cparams("arbitrary"),
        name="pool_layer",
    )(x, x, x, mod, gpre, gpost, pw, ps)


def _rope_tables(seq):
    pos = jnp.arange(seq, dtype=jnp.int32)
    row = (pos // GRID_W).astype(F32)[:, None]
    col = (pos % GRID_W).astype(F32)[:, None]
    axis_dim = HEAD_DIM // 2
    inv = ROPE_THETA ** (-jnp.arange(0, axis_dim, 2, dtype=F32) / axis_dim)
    ar, ac = row * inv, col * inv
    cos = jnp.concatenate([jnp.cos(ar), jnp.cos(ar), jnp.cos(ac), jnp.cos(ac)], axis=-1)
    sin = jnp.concatenate([-jnp.sin(ar), jnp.sin(ar), -jnp.sin(ac), jnp.sin(ac)], axis=-1)
    return cos, sin


def _qkv_kernel(x_ref, mod_ref, g_ref, w0_ref, w1_ref, hg_ref, cos_ref, sin_ref, q_ref, k_ref, vt_ref, h_scr, *,
                n_q_steps):
    j = pl.program_id(1)
    tn = k_ref.shape[1]

    @pl.when(j == 0)
    def _():
        m = mod_ref[0]
        h_scr[...] = _modnorm(x_ref[...], g_ref[...], 1.0 + m[1:2], m[0:1]).astype(BF16)

    def proj(t):
        return jnp.dot(h_scr[...], (w0_ref, w1_ref)[t][...], preferred_element_type=F32)

    def norm_rope(a, gain, out_scale, o_ref, col0):
        cos = cos_ref[...]
        sin = sin_ref[...]
        quarter = HEAD_DIM // 4
        lane = lax.broadcasted_iota(jnp.int32, (1, HEAD_DIM), 1)
        low = (lane % (2 * quarter)) < quarter
        for h in range(tn // HEAD_DIM):
            v = a[:, h * HEAD_DIM:(h + 1) * HEAD_DIM]
            ms = jnp.mean(v * v, axis=-1, keepdims=True)
            v = v * lax.rsqrt(ms + EPS) * gain
            swapped = jnp.where(low, pltpu.roll(v, HEAD_DIM - quarter, 1), pltpu.roll(v, quarter, 1))
            o_ref[:, col0 + h * HEAD_DIM:col0 + (h + 1) * HEAD_DIM] = (
                (v * cos + swapped * sin) * out_scale).astype(o_ref.dtype)

    @pl.when(j < n_q_steps)
    def _():
        for t in range(2):
            norm_rope(proj(t), hg_ref[0:1], HEAD_DIM ** -0.5 * LOG2E, q_ref, t * tn)

    @pl.when(j == n_q_steps)
    def _():
        norm_rope(proj(0), hg_ref[1:2], 1.0, k_ref, 0)
        v = proj(1)
        for c in range(vt_ref.shape[0]):
            vt_ref[c] = v[c * FLASH_TK:(c + 1) * FLASH_TK].T.astype(vt_ref.dtype)


def _qkv(x, mod, gain, w, head_gains, cos, sin, *, seq, n_kv):
    t_rows, d = x.shape
    n_out = w.shape[1]
    tm = _tile(seq, QKV_TM)
    tn = HEAD_DIM * n_kv
    tps = seq // tm
    n_q_steps = d // (2 * tn)
    assert tm % FLASH_TK == 0 and n_out == d + 2 * tn and d % (2 * tn) == 0
    return pl.pallas_call(
        functools.partial(_qkv_kernel, n_q_steps=n_q_steps),
        out_shape=(jax.ShapeDtypeStruct((t_rows, d), BF16), jax.ShapeDtypeStruct((t_rows, tn), BF16),
                   jax.ShapeDtypeStruct((t_rows // FLASH_TK, tn, FLASH_TK), BF16)),
        grid=(t_rows // tm, n_q_steps + 1),
        in_specs=[
            pl.BlockSpec((tm, d), lambda i, j: (i, 0)),
            pl.BlockSpec((1, 6, d), lambda i, j: (i // tps, 0, 0)),
            pl.BlockSpec((1, d), lambda i, j: (0, 0)),
            pl.BlockSpec((d, tn), lambda i, j: (0, 2 * j)),
            pl.BlockSpec((d, tn), lambda i, j: (0, 2 * j + 1)),
            pl.BlockSpec((2, HEAD_DIM), lambda i, j: (0, 0)),
            pl.BlockSpec((tm, HEAD_DIM), lambda i, j: (i % tps, 0)),
            pl.BlockSpec((tm, HEAD_DIM), lambda i, j: (i % tps, 0)),
        ],
        out_specs=(
            pl.BlockSpec((tm, 2 * tn), lambda i, j: (i, jnp.minimum(j, n_q_steps - 1))),
            pl.BlockSpec((tm, tn), lambda i, j: (i, 0)),
            pl.BlockSpec((tm // FLASH_TK, tn, FLASH_TK), lambda i, j: (i, 0, 0)),
        ),
        scratch_shapes=[pltpu.VMEM((tm, d), BF16)],
        compiler_params=_cparams("arbitrary", "arbitrary"),
        name="qkv_rope",
    )(x, mod, gain, w, w, head_gains, cos, sin)


def _flash_kernel(q_ref, k_ref, vt_ref, o_ref, acc_ref, s_ref, *, tq, nk):
    q = q_ref[...]
    qs = jnp.concatenate([q[:, h * HEAD_DIM:(h + 1) * HEAD_DIM] for h in range(GQA_GROUP)], axis=0)
    ones = jnp.ones((ONES_ROWS, FLASH_TK), BF16)
    acc_ref[...] = jnp.zeros_like(acc_ref)

    def scores(kk, slot):
        start = pl.multiple_of(kk * FLASH_TK, FLASH_TK)
        s_ref[slot] = lax.dot_general(k_ref[pl.ds(start, FLASH_TK), :], qs, (((1,), (1,)), ((), ())),
                                      preferred_element_type=F32)

    def accumulate(kk, slot, m):
        st = s_ref[slot]
        m_new = jnp.maximum(m, jnp.max(st, axis=0, keepdims=True))
        alpha = jnp.exp2(m - m_new)
        pt = jnp.exp2(st - m_new).astype(BF16)
        va = jnp.concatenate([vt_ref[kk], ones], axis=0)
        acc_ref[...] = alpha * acc_ref[...] + jnp.dot(va, pt, preferred_element_type=F32)
        return m_new

    scores(0, 0)
    unroll = 4 if nk % 4 == 0 else 2

    def group(i, m):
        k0 = unroll * i
        for u in range(unroll):
            nxt = k0 + u + 1
            scores(nxt if u + 1 < unroll else jnp.minimum(nxt, nk - 1), (u + 1) % 2)
            m = accumulate(k0 + u, u % 2, m)
        return m

    lax.fori_loop(0, nk // unroll, group, jnp.full((1, GQA_GROUP * tq), -jnp.inf, F32))
    acc = acc_ref[...]
    ot = acc[:HEAD_DIM] * (1.0 / acc[HEAD_DIM:HEAD_DIM + 1])
    for h in range(GQA_GROUP):
        o_ref[:, h * HEAD_DIM:(h + 1) * HEAD_DIM] = ot[:, h * tq:(h + 1) * tq].T.astype(o_ref.dtype)


def _flash(q, k, vt, *, batch, seq, n_kv):
    t_rows, d = q.shape
    tq = _tile(seq, FLASH_TQ)
    nqb = seq // tq
    nk = seq // FLASH_TK
    assert nk % 2 == 0
    gw = GQA_GROUP * HEAD_DIM
    return pl.pallas_call(
        functools.partial(_flash_kernel, tq=tq, nk=nk),
        out_shape=jax.ShapeDtypeStruct((t_rows, d), BF16),
        grid=(batch, n_kv, nqb),
        in_specs=[
            pl.BlockSpec((tq, gw), lambda b, kv, qi: (b * nqb + qi, kv)),
            pl.BlockSpec((seq, HEAD_DIM), lambda b, kv, qi: (b, kv)),
            pl.BlockSpec((nk, HEAD_DIM, FLASH_TK), lambda b, kv, qi: (b, kv, 0)),
        ],
        out_specs=pl.BlockSpec((tq, gw), lambda b, kv, qi: (b * nqb + qi, kv)),
        scratch_shapes=[pltpu.VMEM((HEAD_DIM + ONES_ROWS, GQA_GROUP * tq), F32),
                        pltpu.VMEM((2, FLASH_TK, GQA_GROUP * tq), F32)],
        compiler_params=_cparams("arbitrary", "arbitrary", "arbitrary"),
        name="flash_attention",
    )(q, k, vt)


def _dft_sizes(seq):
    n = 2 * seq
    n2 = DFT_N2_MAX
    while n // n2 < DFT_N1_MIN:
        n2 //= 2
    return n, n // n2, n2


def _dft_tables(seq):
    n, n1, n2 = _dft_sizes(seq)
    h2 = n2 // 2
    k2 = jnp.arange(h2, dtype=jnp.int32)
    m2 = jnp.arange(h2, dtype=jnp.int32)
    idx = (m2[None, :] * (2 * k2[:, None] + 1)) % (2 * n2)
    th = idx.astype(F32) * (2.0 * math.pi / (2 * n2))
    m1 = jnp.concatenate([jnp.cos(th), -jnp.sin(th)], axis=0)
    a1 = jnp.arange(n1, dtype=jnp.int32)
    k1 = jnp.arange(n1, dtype=jnp.int32)
    freq = 2 * n2 * k1[None, :, None] + 2 * k2[:, None, None] + 1
    idx = (a1[None, None, :] * freq) % (2 * n)
    ph = idx.astype(F32) * (2.0 * math.pi / (2 * n))
    c, s = jnp.cos(ph), jnp.sin(ph)
    gb = jnp.concatenate([jnp.concatenate([c, s], axis=2), jnp.concatenate([-s, c], axis=2)], axis=1)
    return m1, gb


def _filt_kernel(z_ref, w1_ref, b1_ref, w2_ref, b2_ref, fr_ref, w3_ref, dl_ref, o_ref):
    z = z_ref[...]
    fr = fr_ref[...]
    a = jnp.sin(fr * (_dot3(z, w1_ref[...]) + b1_ref[...]))
    a = jnp.sin(fr * (_dot3(a, w2_ref[...]) + b2_ref[...]))
    decay = jnp.exp(-z[:, 0:1] * dl_ref[...]) + HYENA_MOD_SHIFT
    d = decay.shape[1]
    row = pl.program_id(0) * z.shape[0] + lax.broadcasted_iota(jnp.int32, (z.shape[0], 1), 0)
    for c in range(w3_ref.shape[1] // d):
        f = _dot3(a, w3_ref[:, c * d:(c + 1) * d]) * decay
        o_ref[:, c * d:(c + 1) * d] = jnp.where(row == 0, 0.0, f) if c % 2 == 1 else f


def _filters(z, w1, b1, w2, b2, fr, w3, deltas):
    seq = z.shape[0]
    d = deltas.shape[1]
    tl = _tile(seq, FILT_ROWS)
    hid = w2.shape[0]
    const = lambda i: (0, 0)
    return pl.pallas_call(
        _filt_kernel,
        out_shape=jax.ShapeDtypeStruct((seq, w3.shape[1]), F32),
        grid=(seq // tl,),
        in_specs=[
            pl.BlockSpec((tl, HYENA_EMB_PAD), lambda i: (i, 0)),
            pl.BlockSpec((HYENA_EMB_PAD, hid), const),
            pl.BlockSpec((1, hid), const),
            pl.BlockSpec((hid, hid), const),
            pl.BlockSpec((1, hid), const),
            pl.BlockSpec((1, hid), const),
            pl.BlockSpec(w3.shape, const),
            pl.BlockSpec((1, d), const),
        ],
        out_specs=pl.BlockSpec((tl, w3.shape[1]), lambda i: (i, 0)),
        compiler_params=_cparams("arbitrary"),
        name="hyena_filters",
    )(z, w1, b1, w2, b2, fr, w3, deltas)


def _dft1_kernel(x_ref, m_ref, o_ref, y_scr):
    xt = pltpu.einshape("hjd->jhd", x_ref[0])
    m = m_ref[...]
    for j in range(DFT_ROWS):
        y_scr[j] = _dft_dot(m, xt[j])
    yt = pltpu.einshape("jnd->njd", y_scr[...]).astype(o_ref.dtype)
    h2 = yt.shape[0] // 2
    o_ref[0, 0] = yt[:h2]
    o_ref[0, 1] = yt[h2:]


def _dft1(x4, m1b, *, td, n_blocks, col_map):
    b, h2, n1, _ = x4.shape
    n2 = m1b.shape[0]
    return pl.pallas_call(
        _dft1_kernel,
        out_shape=jax.ShapeDtypeStruct((b, 2, h2, n1, n_blocks * td), BF16),
        grid=(b, n1 // DFT_ROWS, n_blocks),
        in_specs=[
            pl.BlockSpec((1, h2, DFT_ROWS, td), lambda bi, g, c: (bi, 0, g, col_map(c))),
            pl.BlockSpec(m1b.shape, lambda bi, g, c: (0, 0)),
        ],
        out_specs=pl.BlockSpec((1, 2, h2, DFT_ROWS, td), lambda bi, g, c: (bi, 0, 0, g, c)),
        scratch_shapes=[pltpu.VMEM((DFT_ROWS, n2, td), F32)],
        compiler_params=_cparams("arbitrary", "arbitrary", "arbitrary"),
        name="dft_stage1",
    )(x4, m1b)


def _fspec_kernel(af_ref, ab_ref, g_ref, o_ref, *, kb, n1):
    xf = [_dft_dot(g_ref[kk], jnp.concatenate([af_ref[0, kk], af_ref[1, kk]], axis=0)) for kk in range(kb)]
    xb = [_dft_dot(g_ref[kk], jnp.concatenate([ab_ref[0, kk], ab_ref[1, kk]], axis=0)) for kk in range(kb)]
    for kk in range(kb):
        o_ref[0, 0, kk] = xf[kk][:n1] + xb[kk][:n1]
        o_ref[0, 1, kk] = xf[kk][n1:] - xb[kk][n1:]


def _filter_spectrum(af, gbb, *, d, kb, td):
    _, h2, n1, w = af.shape
    ndt = d // td
    norder = w // (2 * d)
    return pl.pallas_call(
        functools.partial(_fspec_kernel, kb=kb, n1=n1),
        out_shape=jax.ShapeDtypeStruct((norder, 2, h2, n1, d), F32),
        grid=(norder, h2 // kb, ndt),
        in_specs=[
            pl.BlockSpec((2, kb, n1, td), lambda o, k, t: (0, k, 0, (2 * o) * ndt + t)),
            pl.BlockSpec((2, kb, n1, td), lambda o, k, t: (0, k, 0, (2 * o + 1) * ndt + t)),
            pl.BlockSpec((kb, 2 * n1, 2 * n1), lambda o, k, t: (k, 0, 0)),
        ],
        out_specs=pl.BlockSpec((1, 2, kb, n1, td), lambda o, k, t: (o, 0, k, 0, t)),
        compiler_params=_cparams("arbitrary", "arbitrary", "arbitrary"),
        name="hyena_filter_spectrum",
    )(af, af, gbb)


def _cmid_kernel(a_ref, ks_ref, g_ref, gt_ref, o_ref, *, kb, n1):
    xs = [_dft_dot(g_ref[kk], jnp.concatenate([a_ref[0, 0, kk], a_ref[0, 1, kk]], axis=0)) for kk in range(kb)]
    ys = []
    for kk in range(kb):
        xr, xi = xs[kk][:n1], xs[kk][n1:]
        kr, ki = ks_ref[0, 0, kk], ks_ref[0, 1, kk]
        ys.append(jnp.concatenate([xr * kr - xi * ki, xr * ki + xi * kr], axis=0))
    bps = [_dft_dot(gt_ref[kk], ys[kk]) for kk in range(kb)]
    for kk in range(kb):
        o_ref[0, 0, kk] = bps[kk][:n1]
        o_ref[0, 1, kk] = bps[kk][n1:]


def _conv_mid(a, kspec, order, gbb, gtb, *, kb, td):
    b, _, h2, n1, d = a.shape
    return pl.pallas_call(
        functools.partial(_cmid_kernel, kb=kb, n1=n1),
        out_shape=jax.ShapeDtypeStruct(a.shape, F32),
        grid=(h2 // kb, d // td, b),
        in_specs=[
            pl.BlockSpec((1, 2, kb, n1, td), lambda k, t, bi: (bi, 0, k, 0, t)),
            pl.BlockSpec((1, 2, kb, n1, td), lambda k, t, bi: (order, 0, k, 0, t)),
            pl.BlockSpec((kb, 2 * n1, 2 * n1), lambda k, t, bi: (k, 0, 0)),
            pl.BlockSpec((kb, 2 * n1, 2 * n1), lambda k, t, bi: (k, 0, 0)),
        ],
        out_specs=pl.BlockSpec((1, 2, kb, n1, td), lambda k, t, bi: (bi, 0, k, 0, t)),
        compiler_params=_cparams("arbitrary", "arbitrary", "arbitrary"),
        name="hyena_conv_mid",
    )(a, kspec, gbb, gtb)


def _idft1_kernel(bp_ref, m_ref, gate_ref, zz_ref, skip_ref, o_ref, y_scr):
    bp = jnp.concatenate([bp_ref[0, 0], bp_ref[0, 1]], axis=0)
    bt = pltpu.einshape("njd->jnd", bp)
    m = m_ref[...]
    for j in range(IDFT_ROWS):
        y_scr[j] = _dft_dot(m, bt[j])
    y = pltpu.einshape("jhd->hjd", y_scr[...])
    o_ref[0] = (gate_ref[0] * (y + zz_ref[0] * skip_ref[...])).astype(o_ref.dtype)


def _idft1_gate(bp, minvb, u4, zz4, skip, *, td, gate_part, zz_part):
    b, _, h2, n1, d = bp.shape
    nt = d // td
    return pl.pallas_call(
        _idft1_kernel,
        out_shape=jax.ShapeDtypeStruct((b, h2, n1, d), F32),
        grid=(b, n1 // IDFT_ROWS, nt),
        in_specs=[
            pl.BlockSpec((1, 2, h2, IDFT_ROWS, td), lambda bi, g, c: (bi, 0, 0, g, c)),
            pl.BlockSpec(minvb.shape, lambda bi, g, c: (0, 0)),
            pl.BlockSpec((1, h2, IDFT_ROWS, td), lambda bi, g, c: (bi, 0, g, gate_part * nt + c)),
            pl.BlockSpec((1, h2, IDFT_ROWS, td), lambda bi, g, c: (bi, 0, g, zz_part * nt + c)),
            pl.BlockSpec((1, td), lambda bi, g, c: (0, c)),
        ],
        out_specs=pl.BlockSpec((1, h2, IDFT_ROWS, td), lambda bi, g, c: (bi, 0, g, c)),
        scratch_shapes=[pltpu.VMEM((IDFT_ROWS, h2, td), F32)],
        compiler_params=_cparams("arbitrary", "arbitrary", "arbitrary"),
        name="idft_stage1_gate",
    )(bp, minvb, u4, zz4, skip)


def _hyena_features(seq, d):
    t = jnp.linspace(0.0, 1.0, seq, dtype=F32)[:, None]
    w = 2.0 * math.pi * jnp.arange(seq, dtype=F32)[:, None] / seq
    f = jnp.linspace(1e-4, HYENA_BANDS - 1, HYENA_BANDS, dtype=F32)[None, :]
    z = jnp.concatenate([t, jnp.cos(f * w), -jnp.sin(f * w)], axis=-1)
    z = jnp.pad(z, ((0, 0), (0, HYENA_EMB_PAD - HYENA_EMB)))
    deltas = jnp.abs(jnp.linspace(HYENA_MIN_DECAY, HYENA_MAX_DECAY, d, dtype=F32))[None, :]
    return z, deltas


def _hyena_mixer(x, mod, gain, p, *, batch, seq):
    t_rows, d = x.shape
    n, n1, n2 = _dft_sizes(seq)
    h2 = n2 // 2
    u = _up(x, mod, gain, p["w_in"][None], p["b_in"], p["conv_w"], p["conv_b"], layer=0, seq=seq, sh_idx=0,
            sc_idx=1, glu=False, out_dtype=F32)

    z, deltas = _hyena_features(seq, d)
    filt = _filters(z, p["f_w1"], p["f_b1"], p["f_w2"], p["f_b2"], p["f_freq"], p["f_w3"], deltas)

    m1, gb = _dft_tables(seq)
    m1b = m1.astype(BF16)
    minvb = (m1.T * (2.0 / n)).astype(BF16)
    gbb = gb.astype(BF16)
    gtb = jnp.swapaxes(gb, 1, 2).astype(BF16)
    kb = max(1, 256 // n1)
    td = _tile(d, MID_LANES)
    ts = _tile(d, DFT_LANES)
    nts = d // ts

    ncf = filt.shape[1]
    af = _dft1(filt.reshape(1, h2, n1, ncf), m1b, td=ts, n_blocks=ncf // ts, col_map=lambda c: c)
    kspec = _filter_spectrum(af[0], gbb, d=d, kb=kb, td=td)

    u4 = u.reshape(batch, h2, n1, 3 * d)
    zz4 = u4
    for o in range(HYENA_ORDER):
        a = _dft1(zz4, m1b, td=ts, n_blocks=nts, col_map=lambda c: c)
        bp = _conv_mid(a, kspec, o, gbb, gtb, kb=kb, td=td)
        zz4 = _idft1_gate(bp, minvb, u4, zz4, p["skip"][o:o + 1], td=ts, gate_part=1 + o, zz_part=0)
    return zz4.reshape(t_rows, d)


def _trunk(x3, modall, row0, p):
    batch, seq, d = x3.shape
    t_rows = batch * seq
    x = x3.reshape(t_rows, d)
    n_kv = d // HEAD_DIM // GQA_GROUP
    depth = modall.shape[0]
    for i in range(depth):
        mod = modall[i, row0:row0 + batch].reshape(batch, 6, d)
        kind, j = i % 3, i // 3
        g_pre, g_post = p["norm_mix_pre"][i:i + 1], p["norm_mix_post"][i:i + 1]
        if kind == 0:
            x = _pool_layer(x, mod, g_pre, g_post, p["pool_w"][j], p["pool_scale"][j:j + 1], seq=seq)
        elif kind == 1:
            hp = {k: v[j] for k, v in p["hy"].items()}
            zz = _hyena_mixer(x, mod, g_pre, hp, batch=batch, seq=seq)
            x = _down(zz, p["hy"]["w_out"], x, mod, g_post, layer=j, seq=seq, gate_idx=2)
        else:
            cos, sin = _rope_tables(seq)
            gains = jnp.stack([p["at_q_gain"][j], p["at_k_gain"][j]], axis=0)
            q, k, vt = _qkv(x, mod, g_pre, p["at_w_qkv"][j], gains, cos, sin, seq=seq, n_kv=n_kv)
            o = _flash(q, k, vt, batch=batch, seq=seq, n_kv=n_kv)
            x = _down(o, p["at_w_o"], x, mod, g_post, layer=j, seq=seq, gate_idx=2)
        g = _up(x, mod, p["norm_ffn_pre"][i:i + 1], p["ffn_w_up"], None, p["ffn_conv_w"][i],
                p["ffn_conv_b"][i:i + 1], layer=i, seq=seq, sh_idx=3, sc_idx=4, glu=True, out_dtype=BF16)
        x = _down(g, p["ffn_w_down"], x, mod, p["norm_ffn_post"][i:i + 1], layer=i, seq=seq, gate_idx=5)
    return x.reshape(batch, seq, d)


def kernel(x_prompt, x_sample, c_prompt, c_sample, mod_w, mod_b, norm_mix_pre, norm_mix_post, norm_ffn_pre,
           norm_ffn_post, ffn_w_up, ffn_conv_w, ffn_conv_b, ffn_w_down, pool_w, pool_scale, hy_w_in, hy_b_in,
           hy_conv_w, hy_conv_b, hy_f_w1, hy_f_b1, hy_f_w2, hy_f_b2, hy_f_w3, hy_f_freq, hy_skip, hy_w_out,
           at_w_qkv, at_q_gain, at_k_gain, at_w_o):
    nb = c_prompt.shape[0]
    ns = c_sample.shape[0]
    assert nb + ns <= MOD_ROWS
    c8 = jnp.concatenate([c_prompt, c_sample, jnp.zeros((MOD_ROWS - nb - ns, c_prompt.shape[1]), F32)], axis=0)
    modall = _modulation(c8, mod_w, mod_b)
    row = lambda v: v[:, None, :]
    p = dict(
        norm_mix_pre=norm_mix_pre, norm_mix_post=norm_mix_post, norm_ffn_pre=norm_ffn_pre,
        norm_ffn_post=norm_ffn_post,
        ffn_w_up=ffn_w_up.astype(BF16), ffn_conv_w=ffn_conv_w, ffn_conv_b=ffn_conv_b,
        ffn_w_down=ffn_w_down.astype(BF16),
        pool_w=pool_w.astype(BF16), pool_scale=pool_scale,
        hy=dict(
            w_in=hy_w_in.astype(BF16), b_in=row(hy_b_in), conv_w=hy_conv_w, conv_b=row(hy_conv_b),
            f_w1=jnp.pad(hy_f_w1, ((0, 0), (0, HYENA_EMB_PAD - HYENA_EMB), (0, 0))), f_b1=row(hy_f_b1),
            f_w2=hy_f_w2, f_b2=row(hy_f_b2), f_w3=hy_f_w3, f_freq=row(hy_f_freq), skip=hy_skip,
            w_out=hy_w_out.astype(BF16)),
        at_w_qkv=at_w_qkv.astype(BF16), at_q_gain=at_q_gain, at_k_gain=at_k_gain, at_w_o=at_w_o.astype(BF16),
    )
    y_prompt = _trunk(x_prompt, modall, 0, p)
    y_sample = _trunk(x_sample, modall, nb, p)
    return (y_prompt, y_sample)
```

```python
import functools
import math

import jax
import jax.numpy as jnp
from jax import lax
from jax.experimental import pallas as pl
from jax.experimental.pallas import tpu as pltpu

EPS = 1e-6
HEAD_DIM = 128
GQA_GROUP = 4
GRID_W = 64
ROPE_THETA = 10000.0
POOL_WINDOWS = (2, 4, 8, 16)
HYENA_ORDER = 2
HYENA_EMB = 33
HYENA_BANDS = (HYENA_EMB - 1) // 2
HYENA_EMB_PAD = 64
HYENA_FAST_DECAY = 0.3
HYENA_SLOW_DECAY = 1.5
HYENA_TARGET = 1e-2
HYENA_MOD_SHIFT = 0.05
HYENA_MIN_DECAY = math.log(HYENA_TARGET) / HYENA_SLOW_DECAY
HYENA_MAX_DECAY = math.log(HYENA_TARGET) / HYENA_FAST_DECAY

HALO = 16
MOD_ROWS = 8
DFT_N2_MAX = 512
DFT_N1_MIN = 16
DFT_ROWS = 16
IDFT_ROWS = 8
LANES = 128
UP_SUB = 512
FLASH_TK = 512
ONES_ROWS = 16
LOG2E = 1.4426950408889634
VMEM_LIMIT = 58 * 1024 * 1024

UP_TM = 1024
DOWN_TM = 512
DOWN_TK_MAX = 2816
POOL_TM = 256
QKV_TM = 512
FLASH_TQ = 512
MOD_TN = 1024
FILT_ROWS = 256
MID_LANES = 1024
DFT_LANES = 256

F32 = jnp.float32
BF16 = jnp.bfloat16


def _cparams(*sem):
    return pltpu.CompilerParams(dimension_semantics=sem, vmem_limit_bytes=VMEM_LIMIT)


def _tile(dim, pref):
    t = min(dim, pref)
    while dim % t:
        t //= 2
    return t


def _lane_tile(dim, cap):
    units = dim // LANES
    best = max(u for u in range(1, units + 1) if units % u == 0 and u * LANES <= max(cap, LANES))
    return best * LANES


def _split3(x):
    hi = x.astype(BF16)
    lo = (x - hi.astype(F32)).astype(BF16)
    return hi, lo


def _dot3(a, b):
    ah, al = _split3(a)
    bh, bl = _split3(b)
    d = functools.partial(jnp.dot, preferred_element_type=F32)
    return d(ah, bh) + (d(ah, bl) + d(al, bh))


def _dft_dot(m, x):
    return jnp.dot(m, x.astype(BF16), preferred_element_type=F32)


def _modnorm(xv, gain, scale1p, shift):
    ms = jnp.mean(xv * xv, axis=-1, keepdims=True)
    return xv * lax.rsqrt(ms + EPS) * gain * scale1p + shift


def _halo_maps(tm, t_rows):
    r = tm // HALO
    last = t_rows // HALO - 1
    prev = lambda i, *_: (jnp.maximum(i * r - 1, 0), 0)
    nxt = lambda i, *_: (jnp.minimum((i + 1) * r, last), 0)
    return prev, nxt


def _mod_kernel(c_ref, w_ref, b_ref, o_ref):
    c = c_ref[...]
    a = (c / (1.0 + jnp.exp(-c))).astype(BF16)
    o_ref[0] = jnp.dot(a, w_ref[0].astype(BF16), preferred_element_type=F32) + b_ref[0]


def _modulation(c8, mod_w, mod_b):
    depth, d, n = mod_w.shape
    tn = _tile(n, MOD_TN)
    return pl.pallas_call(
        _mod_kernel,
        out_shape=jax.ShapeDtypeStruct((depth, MOD_ROWS, n), F32),
        grid=(depth, n // tn),
        in_specs=[
            pl.BlockSpec((MOD_ROWS, d), lambda l, j: (0, 0)),
            pl.BlockSpec((1, d, tn), lambda l, j: (l, 0, j)),
            pl.BlockSpec((1, 1, tn), lambda l, j: (l, 0, j)),
        ],
        out_specs=pl.BlockSpec((1, MOD_ROWS, tn), lambda l, j: (l, 0, j)),
        compiler_params=_cparams("arbitrary", "arbitrary"),
        name="modulation",
    )(c8, mod_w, mod_b.reshape(depth, 1, n))


def _fill_h(h_scr, x_ref, xp_ref, xn_ref, mod_ref, g_ref, sh_idx, sc_idx, tm):
    m = mod_ref[0]
    shift = m[sh_idx:sh_idx + 1]
    scale1p = 1.0 + m[sc_idx:sc_idx + 1]
    gain = g_ref[...]
    h_scr[0:HALO] = _modnorm(xp_ref[...], gain, scale1p, shift).astype(h_scr.dtype)
    h_scr[HALO:HALO + tm] = _modnorm(x_ref[...], gain, scale1p, shift).astype(h_scr.dtype)
    h_scr[HALO + tm:] = _modnorm(xn_ref[...], gain, scale1p, shift).astype(h_scr.dtype)


def _edge_keep(tps):
    i = pl.program_id(0) % tps
    return (i != 0).astype(F32), (i != tps - 1).astype(F32)


def _conv3_rows(a, keep_first, keep_last, cw, cb, tm):
    rows = tm + 2 * HALO
    a = jnp.concatenate([a[0:HALO] * keep_first, a[HALO:HALO + tm], a[HALO + tm:] * keep_last], axis=0)
    prev = pltpu.roll(a, 1, 0)
    nxt = pltpu.roll(a, rows - 1, 0)
    c = prev * cw[0:1] + a * cw[1:2] + nxt * cw[2:3] + cb
    return c[HALO:HALO + tm]


def _gelu_tanh(x):
    return 0.5 * x * (1.0 + jnp.tanh(0.7978845608028654 * (x + 0.044715 * (x * x * x))))


def _up_kernel(*refs, sh_idx, sc_idx, tm, tps, glu, has_bias, n_chunks):
    x_ref, xp_ref, xn_ref, mod_ref, g_ref, w_ref = refs[:6]
    k = 6
    wb_ref = b_ref = None
    if glu:
        wb_ref = refs[k]
        k += 1
    if has_bias:
        b_ref = refs[k]
        k += 1
    cw_ref, cb_ref, o_ref, h_scr, hn_scr = refs[k:k + 5]
    i = pl.program_id(0)
    j = pl.program_id(1)

    @pl.when(jnp.logical_and(j == 0, i == 0))
    def _():
        _fill_h(h_scr, x_ref, xp_ref, xn_ref, mod_ref, g_ref, sh_idx, sc_idx, tm)

    @pl.when(jnp.logical_and(j == 0, i > 0))
    def _():
        h_scr[...] = hn_scr[...]

    m = mod_ref[0]
    shift = m[sh_idx:sh_idx + 1]
    scale1p = 1.0 + m[sc_idx:sc_idx + 1]
    gain = g_ref[...]

    @pl.when(j == 1)
    def _():
        hn_scr[0:HALO] = _modnorm(xp_ref[...], gain, scale1p, shift).astype(hn_scr.dtype)
        hn_scr[HALO + tm:] = _modnorm(xn_ref[...], gain, scale1p, shift).astype(hn_scr.dtype)

    rc = tm // n_chunks
    chunk = jnp.clip(j - 1, 0, n_chunks - 1)
    r0 = pl.multiple_of(chunk * rc, HALO)
    hn_scr[pl.ds(pl.multiple_of(HALO + r0, HALO), rc), :] = _modnorm(
        x_ref[pl.ds(r0, rc), :], gain, scale1p, shift).astype(hn_scr.dtype)

    keep_first, keep_last = _edge_keep(tps)
    for s in range(o_ref.shape[1] // UP_SUB):
        sl = slice(s * UP_SUB, (s + 1) * UP_SUB)
        a = jnp.dot(h_scr[...], w_ref[:, sl], preferred_element_type=F32)
        if has_bias:
            a = a + b_ref[:, sl]
        c = _conv3_rows(a, keep_first, keep_last, cw_ref[:, sl], cb_ref[:, sl], tm)
        if glu:
            b = jnp.dot(h_scr[HALO:HALO + tm], wb_ref[:, sl], preferred_element_type=F32)
            c = _gelu_tanh(c) * b
        o_ref[:, sl] = c.astype(o_ref.dtype)


def _up(x, mod, gain, w, bias, cw, cb, *, layer, seq, sh_idx, sc_idx, glu, out_dtype):
    t_rows, d = x.shape
    n_out = cw.shape[-1]
    tm = _tile(seq, UP_TM)
    tn = _tile(n_out, UP_SUB if glu else 2 * UP_SUB)
    assert tn % UP_SUB == 0
    tps = seq // tm
    nj = n_out // tn
    n_tiles = t_rows // tm
    assert nj >= 2
    n_chunks = 1
    while 2 * n_chunks <= nj - 1 and tm % (2 * n_chunks * HALO) == 0:
        n_chunks *= 2
    r = tm // HALO

    def ahead(i, j):
        return jnp.minimum(i + jnp.where(jnp.logical_and(i == 0, j == 0), 0, 1), n_tiles - 1)

    in_specs = [
        pl.BlockSpec((tm, d), lambda i, j: (ahead(i, j), 0)),
        pl.BlockSpec((HALO, d), lambda i, j: (jnp.maximum(ahead(i, j) * r - 1, 0), 0)),
        pl.BlockSpec((HALO, d), lambda i, j: (jnp.minimum((ahead(i, j) + 1) * r, t_rows // HALO - 1), 0)),
        pl.BlockSpec((1, 6, d), lambda i, j: (ahead(i, j) // tps, 0, 0)),
        pl.BlockSpec((1, d), lambda i, j: (0, 0)),
        pl.BlockSpec((None, d, tn), lambda i, j: (layer, 0, j)),
    ]
    args = [x, x, x, mod, gain, w]
    if glu:
        in_specs.append(pl.BlockSpec((None, d, tn), lambda i, j: (layer, 0, j + nj)))
        args.append(w)
    if bias is not None:
        in_specs.append(pl.BlockSpec((1, tn), lambda i, j: (0, j)))
        args.append(bias)
    in_specs += [pl.BlockSpec((3, tn), lambda i, j: (0, j)), pl.BlockSpec((1, tn), lambda i, j: (0, j))]
    args += [cw, cb]
    return pl.pallas_call(
        functools.partial(_up_kernel, sh_idx=sh_idx, sc_idx=sc_idx, tm=tm, tps=tps, glu=glu,
                          has_bias=bias is not None, n_chunks=n_chunks),
        out_shape=jax.ShapeDtypeStruct((t_rows, n_out), out_dtype),
        grid=(n_tiles, nj),
        in_specs=in_specs,
        out_specs=pl.BlockSpec((tm, tn), lambda i, j: (i, j)),
        scratch_shapes=[pltpu.VMEM((tm + 2 * HALO, d), BF16), pltpu.VMEM((tm + 2 * HALO, d), BF16)],
        compiler_params=_cparams("arbitrary", "arbitrary"),
        name="up_glu" if glu else "up_conv",
    )(*args)


def _down_kernel(g_ref, w_ref, x_ref, mod_ref, gain_ref, o_ref, *scratch, gate_idx, nk):
    k = pl.program_id(1)
    acc_ref = scratch[0] if nk > 1 else None
    part = jnp.dot(g_ref[...].astype(w_ref.dtype), w_ref[...], preferred_element_type=F32)

    if nk > 1:
        @pl.when(k == 0)
        def _():
            acc_ref[...] = part

    if nk > 2:
        @pl.when(jnp.logical_and(k > 0, k < nk - 1))
        def _():
            acc_ref[...] += part

    @pl.when(k == nk - 1)
    def _():
        f = acc_ref[...] + part if nk > 1 else part
        ms = jnp.mean(f * f, axis=-1, keepdims=True)
        y = f * lax.rsqrt(ms + EPS) * gain_ref[...]
        gate = mod_ref[0][gate_idx:gate_idx + 1]
        o_ref[...] = x_ref[...] + gate * y


def _down(g, w, x, mod, gain, *, layer, seq, gate_idx):
    t_rows, kdim = g.shape
    d = w.shape[2]
    tm = _tile(seq, DOWN_TM)
    tk = _lane_tile(kdim, DOWN_TK_MAX)
    tps = seq // tm
    nk = kdim // tk
    return pl.pallas_call(
        functools.partial(_down_kernel, gate_idx=gate_idx, nk=nk),
        out_shape=jax.ShapeDtypeStruct((t_rows, d), F32),
        grid=(t_rows // tm, nk),
        in_specs=[
            pl.BlockSpec((tm, tk), lambda i, k: (i, k)),
            pl.BlockSpec((None, tk, d), lambda i, k: (layer, k, 0)),
            pl.BlockSpec((tm, d), lambda i, k: (i, 0)),
            pl.BlockSpec((1, 6, d), lambda i, k: (i // tps, 0, 0)),
            pl.BlockSpec((1, d), lambda i, k: (0, 0)),
        ],
        out_specs=pl.BlockSpec((tm, d), lambda i, k: (i, 0)),
        scratch_shapes=[pltpu.VMEM((tm, d), F32)] if nk > 1 else [],
        compiler_params=_cparams("arbitrary", "arbitrary"),
        name="down",
    )(g, w, x, mod, gain)


def _pool_kernel(x_ref, xp_ref, xn_ref, mod_ref, gpre_ref, gpost_ref, pw_ref, ps_ref, o_ref, h_scr,
                 *, tm, tps, seq):
    _fill_h(h_scr, x_ref, xp_ref, xn_ref, mod_ref, gpre_ref, 0, 1, tm)
    keep_first, keep_last = _edge_keep(tps)
    rows = tm + 2 * HALO
    d = x_ref.shape[1]
    cg = d // len(POOL_WINDOWS)
    pos = (pl.program_id(0) % tps) * tm + lax.broadcasted_iota(jnp.int32, (tm, 1), 0)
    ys = []
    ssq = jnp.zeros((tm, 1), F32)
    for g, win in enumerate(POOL_WINDOWS):
        half = win // 2
        sl = slice(g * cg, (g + 1) * cg)
        hg = jnp.concatenate([h_scr[0:HALO, sl] * keep_first, h_scr[HALO:HALO + tm, sl],
                              h_scr[HALO + tm:, sl] * keep_last], axis=0)
        p = hg
        s = 1
        while s < win:
            p = p + pltpu.roll(p, s, 0)
            s *= 2
        if half > 1:
            p = pltpu.roll(p, rows - (half - 1), 0)
        lo = jnp.maximum(pos - half, 0)
        hi = jnp.minimum(pos + (half - 1), seq - 1)
        inv_cnt = 1.0 / (hi - lo + 1).astype(F32)
        pooled = p[HALO:HALO + tm] * inv_cnt - hg[HALO:HALO + tm]
        y = jnp.dot(pooled.astype(BF16), pw_ref[g], preferred_element_type=F32) * ps_ref[:, sl]
        ssq = ssq + jnp.sum(y * y, axis=-1, keepdims=True)
        ys.append(y)
    inv = lax.rsqrt(ssq / d + EPS)
    gate = mod_ref[0][2:3]
    for g in range(len(POOL_WINDOWS)):
        sl = slice(g * cg, (g + 1) * cg)
        o_ref[:, sl] = x_ref[:, sl] + gate[:, sl] * (ys[g] * inv * gpost_ref[:, sl])


def _pool_layer(x, mod, gpre, gpost, pw, ps, *, seq):
    t_rows, d = x.shape
    tm = _tile(seq, POOL_TM)
    tps = seq // tm
    prev, nxt = _halo_maps(tm, t_rows)
    return pl.pallas_call(
        functools.partial(_pool_kernel, tm=tm, tps=tps, seq=seq),
        out_shape=jax.ShapeDtypeStruct((t_rows, d), F32),
        grid=(t_rows // tm,),
        in_specs=[
            pl.BlockSpec((tm, d), lambda i: (i, 0)),
            pl.BlockSpec((HALO, d), prev),
            pl.BlockSpec((HALO, d), nxt),
            pl.BlockSpec((1, 6, d), lambda i: (i // tps, 0, 0)),
            pl.BlockSpec((1, d), lambda i: (0, 0)),
            pl.BlockSpec((1, d), lambda i: (0, 0)),
            pl.BlockSpec(pw.shape, lambda i: (0, 0, 0)),
            pl.BlockSpec((1, d), lambda i: (0, 0)),
        ],
        out_specs=pl.BlockSpec((tm, d), lambda i: (i, 0)),
        scratch_shapes=[pltpu.VMEM((tm + 2 * HALO, d), F32)],
        compiler_params=_cparams("arbitrary"),
        name="pool_layer",
    )(x, x, x, mod, gpre, gpost, pw, ps)


def _rope_tables(seq):
    pos = jnp.arange(seq, dtype=jnp.int32)
    row = (pos // GRID_W).astype(F32)[:, None]
    col = (pos % GRID_W).astype(F32)[:, None]
    axis_dim = HEAD_DIM // 2
    inv = ROPE_THETA ** (-jnp.arange(0, axis_dim, 2, dtype=F32) / axis_dim)
    ar, ac = row * inv, col * inv
    cos = jnp.concatenate([jnp.cos(ar), jnp.cos(ar), jnp.cos(ac), jnp.cos(ac)], axis=-1)
    sin = jnp.concatenate([-jnp.sin(ar), jnp.sin(ar), -jnp.sin(ac), jnp.sin(ac)], axis=-1)
    return cos, sin


def _qkv_kernel(x_ref, mod_ref, g_ref, w0_ref, w1_ref, hg_ref, cos_ref, sin_ref, q_ref, k_ref, vt_ref, h_scr, *,
                n_q_steps):
    j = pl.program_id(1)
    tn = k_ref.shape[1]

    @pl.when(j == 0)
    def _():
        m = mod_ref[0]
        h_scr[...] = _modnorm(x_ref[...], g_ref[...], 1.0 + m[1:2], m[0:1]).astype(BF16)

    def proj(t):
        return jnp.dot(h_scr[...], (w0_ref, w1_ref)[t][...], preferred_element_type=F32)

    def norm_rope(a, gain, out_scale, o_ref, col0):
        cos = cos_ref[...]
        sin = sin_ref[...]
        quarter = HEAD_DIM // 4
        lane = lax.broadcasted_iota(jnp.int32, (1, HEAD_DIM), 1)
        low = (lane % (2 * quarter)) < quarter
        for h in range(tn // HEAD_DIM):
            v = a[:, h * HEAD_DIM:(h + 1) * HEAD_DIM]
            ms = jnp.mean(v * v, axis=-1, keepdims=True)
            v = v * lax.rsqrt(ms + EPS) * gain
            swapped = jnp.where(low, pltpu.roll(v, HEAD_DIM - quarter, 1), pltpu.roll(v, quarter, 1))
            o_ref[:, col0 + h * HEAD_DIM:col0 + (h + 1) * HEAD_DIM] = (
                (v * cos + swapped * sin) * out_scale).astype(o_ref.dtype)

    @pl.when(j < n_q_steps)
    def _():
        for t in range(2):
            norm_rope(proj(t), hg_ref[0:1], HEAD_DIM ** -0.5 * LOG2E, q_ref, t * tn)

    @pl.when(j == n_q_steps)
    def _():
        norm_rope(proj(0), hg_ref[1:2], 1.0, k_ref, 0)
        v = proj(1)
        for c in range(vt_ref.shape[0]):
            vt_ref[c] = v[c * FLASH_TK:(c + 1) * FLASH_TK].T.astype(vt_ref.dtype)


def _qkv(x, mod, gain, w, head_gains, cos, sin, *, seq, n_kv):
    t_rows, d = x.shape
    n_out = w.shape[1]
    tm = _tile(seq, QKV_TM)
    tn = HEAD_DIM * n_kv
    tps = seq // tm
    n_q_steps = d // (2 * tn)
    assert tm % FLASH_TK == 0 and n_out == d + 2 * tn and d % (2 * tn) == 0
    return pl.pallas_call(
        functools.partial(_qkv_kernel, n_q_steps=n_q_steps),
        out_shape=(jax.ShapeDtypeStruct((t_rows, d), BF16), jax.ShapeDtypeStruct((t_rows, tn), BF16),
                   jax.ShapeDtypeStruct((t_rows // FLASH_TK, tn, FLASH_TK), BF16)),
        grid=(t_rows // tm, n_q_steps + 1),
        in_specs=[
            pl.BlockSpec((tm, d), lambda i, j: (i, 0)),
            pl.BlockSpec((1, 6, d), lambda i, j: (i // tps, 0, 0)),
            pl.BlockSpec((1, d), lambda i, j: (0, 0)),
            pl.BlockSpec((d, tn), lambda i, j: (0, 2 * j)),
            pl.BlockSpec((d, tn), lambda i, j: (0, 2 * j + 1)),
            pl.BlockSpec((2, HEAD_DIM), lambda i, j: (0, 0)),
            pl.BlockSpec((tm, HEAD_DIM), lambda i, j: (i % tps, 0)),
            pl.BlockSpec((tm, HEAD_DIM), lambda i, j: (i % tps, 0)),
        ],
        out_specs=(
            pl.BlockSpec((tm, 2 * tn), lambda i, j: (i, jnp.minimum(j, n_q_steps - 1))),
            pl.BlockSpec((tm, tn), lambda i, j: (i, 0)),
            pl.BlockSpec((tm // FLASH_TK, tn, FLASH_TK), lambda i, j: (i, 0, 0)),
        ),
        scratch_shapes=[pltpu.VMEM((tm, d), BF16)],
        compiler_params=_cparams("arbitrary", "arbitrary"),
        name="qkv_rope",
    )(x, mod, gain, w, w, head_gains, cos, sin)


def _flash_kernel(q_ref, k_ref, vt_ref, o_ref, acc_ref, s_ref, *, tq, nk):
    q = q_ref[...]
    qs = jnp.concatenate([q[:, h * HEAD_DIM:(h + 1) * HEAD_DIM] for h in range(GQA_GROUP)], axis=0)
    ones = jnp.ones((ONES_ROWS, FLASH_TK), BF16)
    acc_ref[...] = jnp.zeros_like(acc_ref)

    def scores(kk, slot):
        start = pl.multiple_of(kk * FLASH_TK, FLASH_TK)
        s_ref[slot] = lax.dot_general(k_ref[pl.ds(start, FLASH_TK), :], qs, (((1,), (1,)), ((), ())),
                                      preferred_element_type=F32)

    def accumulate(kk, slot, m):
        st = s_ref[slot]
        m_new = jnp.maximum(m, jnp.max(st, axis=0, keepdims=True))
        alpha = jnp.exp2(m - m_new)
        pt = jnp.exp2(st - m_new).astype(BF16)
        va = jnp.concatenate([vt_ref[kk], ones], axis=0)
        acc_ref[...] = alpha * acc_ref[...] + jnp.dot(va, pt, preferred_element_type=F32)
        return m_new

    scores(0, 0)
    unroll = 4 if nk % 4 == 0 else 2

    def group(i, m):
        k0 = unroll * i
        for u in range(unroll):
            nxt = k0 + u + 1
            scores(nxt if u + 1 < unroll else jnp.minimum(nxt, nk - 1), (u + 1) % 2)
            m = accumulate(k0 + u, u % 2, m)
        return m

    lax.fori_loop(0, nk // unroll, group, jnp.full((1, GQA_GROUP * tq), -jnp.inf, F32))
    acc = acc_ref[...]
    ot = acc[:HEAD_DIM] / acc[HEAD_DIM:HEAD_DIM + 1]
    for h in range(GQA_GROUP):
        o_ref[:, h * HEAD_DIM:(h + 1) * HEAD_DIM] = ot[:, h * tq:(h + 1) * tq].T.astype(o_ref.dtype)


def _flash(q, k, vt, *, batch, seq, n_kv):
    t_rows, d = q.shape
    tq = _tile(seq, FLASH_TQ)
    nqb = seq // tq
    nk = seq // FLASH_TK
    assert nk % 2 == 0
    gw = GQA_GROUP * HEAD_DIM
    return pl.pallas_call(
        functools.partial(_flash_kernel, tq=tq, nk=nk),
        out_shape=jax.ShapeDtypeStruct((t_rows, d), BF16),
        grid=(batch, n_kv, nqb),
        in_specs=[
            pl.BlockSpec((tq, gw), lambda b, kv, qi: (b * nqb + qi, kv)),
            pl.BlockSpec((seq, HEAD_DIM), lambda b, kv, qi: (b, kv)),
            pl.BlockSpec((nk, HEAD_DIM, FLASH_TK), lambda b, kv, qi: (b, kv, 0)),
        ],
        out_specs=pl.BlockSpec((tq, gw), lambda b, kv, qi: (b * nqb + qi, kv)),
        scratch_shapes=[pltpu.VMEM((HEAD_DIM + ONES_ROWS, GQA_GROUP * tq), F32),
                        pltpu.VMEM((2, FLASH_TK, GQA_GROUP * tq), F32)],
        compiler_params=_cparams("arbitrary", "arbitrary", "arbitrary"),
        name="flash_attention",
    )(q, k, vt)


def _dft_sizes(seq):
    n = 2 * seq
    n2 = DFT_N2_MAX
    while n // n2 < DFT_N1_MIN:
        n2 //= 2
    return n, n // n2, n2


def _dft_tables(seq):
    n, n1, n2 = _dft_sizes(seq)
    h2 = n2 // 2
    k2 = jnp.arange(h2, dtype=jnp.int32)
    m2 = jnp.arange(h2, dtype=jnp.int32)
    idx = (m2[None, :] * (2 * k2[:, None] + 1)) % (2 * n2)
    th = idx.astype(F32) * (2.0 * math.pi / (2 * n2))
    m1 = jnp.concatenate([jnp.cos(th), -jnp.sin(th)], axis=0)
    a1 = jnp.arange(n1, dtype=jnp.int32)
    k1 = jnp.arange(n1, dtype=jnp.int32)
    freq = 2 * n2 * k1[None, :, None] + 2 * k2[:, None, None] + 1
    idx = (a1[None, None, :] * freq) % (2 * n)
    ph = idx.astype(F32) * (2.0 * math.pi / (2 * n))
    c, s = jnp.cos(ph), jnp.sin(ph)
    gb = jnp.concatenate([jnp.concatenate([c, s], axis=2), jnp.concatenate([-s, c], axis=2)], axis=1)
    return m1, gb


def _filt_kernel(z_ref, w1_ref, b1_ref, w2_ref, b2_ref, fr_ref, w3_ref, dl_ref, o_ref):
    z = z_ref[...]
    fr = fr_ref[...]
    a = jnp.sin(fr * (_dot3(z, w1_ref[...]) + b1_ref[...]))
    a = jnp.sin(fr * (_dot3(a, w2_ref[...]) + b2_ref[...]))
    decay = jnp.exp(-z[:, 0:1] * dl_ref[...]) + HYENA_MOD_SHIFT
    d = decay.shape[1]
    row = pl.program_id(0) * z.shape[0] + lax.broadcasted_iota(jnp.int32, (z.shape[0], 1), 0)
    for c in range(w3_ref.shape[1] // d):
        f = _dot3(a, w3_ref[:, c * d:(c + 1) * d]) * decay
        o_ref[:, c * d:(c + 1) * d] = jnp.where(row == 0, 0.0, f) if c % 2 == 1 else f


def _filters(z, w1, b1, w2, b2, fr, w3, deltas):
    seq = z.shape[0]
    d = deltas.shape[1]
    tl = _tile(seq, FILT_ROWS)
    hid = w2.shape[0]
    const = lambda i: (0, 0)
    return pl.pallas_call(
        _filt_kernel,
        out_shape=jax.ShapeDtypeStruct((seq, w3.shape[1]), F32),
        grid=(seq // tl,),
        in_specs=[
            pl.BlockSpec((tl, HYENA_EMB_PAD), lambda i: (i, 0)),
            pl.BlockSpec((HYENA_EMB_PAD, hid), const),
            pl.BlockSpec((1, hid), const),
            pl.BlockSpec((hid, hid), const),
            pl.BlockSpec((1, hid), const),
            pl.BlockSpec((1, hid), const),
            pl.BlockSpec(w3.shape, const),
            pl.BlockSpec((1, d), const),
        ],
        out_specs=pl.BlockSpec((tl, w3.shape[1]), lambda i: (i, 0)),
        compiler_params=_cparams("arbitrary"),
        name="hyena_filters",
    )(z, w1, b1, w2, b2, fr, w3, deltas)


def _dft1_kernel(x_ref, m_ref, o_ref, y_scr):
    xt = pltpu.einshape("hjd->jhd", x_ref[0])
    m = m_ref[...]
    for j in range(DFT_ROWS):
        y_scr[j] = _dft_dot(m, xt[j])
    yt = pltpu.einshape("jnd->njd", y_scr[...]).astype(o_ref.dtype)
    h2 = yt.shape[0] // 2
    o_ref[0, 0] = yt[:h2]
    o_ref[0, 1] = yt[h2:]


def _dft1(x4, m1b, *, td, n_blocks, col_map):
    b, h2, n1, _ = x4.shape
    n2 = m1b.shape[0]
    return pl.pallas_call(
        _dft1_kernel,
        out_shape=jax.ShapeDtypeStruct((b, 2, h2, n1, n_blocks * td), BF16),
        grid=(b, n1 // DFT_ROWS, n_blocks),
        in_specs=[
            pl.BlockSpec((1, h2, DFT_ROWS, td), lambda bi, g, c: (bi, 0, g, col_map(c))),
            pl.BlockSpec(m1b.shape, lambda bi, g, c: (0, 0)),
        ],
        out_specs=pl.BlockSpec((1, 2, h2, DFT_ROWS, td), lambda bi, g, c: (bi, 0, 0, g, c)),
        scratch_shapes=[pltpu.VMEM((DFT_ROWS, n2, td), F32)],
        compiler_params=_cparams("arbitrary", "arbitrary", "arbitrary"),
        name="dft_stage1",
    )(x4, m1b)


def _fspec_kernel(af_ref, ab_ref, g_ref, o_ref, *, kb, n1):
    xf = [_dft_dot(g_ref[kk], jnp.concatenate([af_ref[0, kk], af_ref[1, kk]], axis=0)) for kk in range(kb)]
    xb = [_dft_dot(g_ref[kk], jnp.concatenate([ab_ref[0, kk], ab_ref[1, kk]], axis=0)) for kk in range(kb)]
    for kk in range(kb):
        o_ref[0, 0, kk] = (xf[kk][:n1] + xb[kk][:n1]).astype(o_ref.dtype)
        o_ref[0, 1, kk] = (xf[kk][n1:] - xb[kk][n1:]).astype(o_ref.dtype)


def _filter_spectrum(af, gbb, *, d, kb, td):
    _, h2, n1, w = af.shape
    ndt = d // td
    norder = w // (2 * d)
    return pl.pallas_call(
        functools.partial(_fspec_kernel, kb=kb, n1=n1),
        out_shape=jax.ShapeDtypeStruct((norder, 2, h2, n1, d), BF16),
        grid=(norder, h2 // kb, ndt),
        in_specs=[
            pl.BlockSpec((2, kb, n1, td), lambda o, k, t: (0, k, 0, (2 * o) * ndt + t)),
            pl.BlockSpec((2, kb, n1, td), lambda o, k, t: (0, k, 0, (2 * o + 1) * ndt + t)),
            pl.BlockSpec((kb, 2 * n1, 2 * n1), lambda o, k, t: (k, 0, 0)),
        ],
        out_specs=pl.BlockSpec((1, 2, kb, n1, td), lambda o, k, t: (o, 0, k, 0, t)),
        compiler_params=_cparams("arbitrary", "arbitrary", "arbitrary"),
        name="hyena_filter_spectrum",
    )(af, af, gbb)


def _cmid_kernel(a_ref, ks_ref, g_ref, gt_ref, o_ref, *, kb, n1):
    xs = [_dft_dot(g_ref[kk], jnp.concatenate([a_ref[0, 0, kk], a_ref[0, 1, kk]], axis=0)) for kk in range(kb)]
    ys = []
    for kk in range(kb):
        xr, xi = xs[kk][:n1], xs[kk][n1:]
        kr, ki = ks_ref[0, 0, kk], ks_ref[0, 1, kk]
        ys.append(jnp.concatenate([xr * kr - xi * ki, xr * ki + xi * kr], axis=0))
    bps = [_dft_dot(gt_ref[kk], ys[kk]) for kk in range(kb)]
    for kk in range(kb):
        o_ref[0, 0, kk] = bps[kk][:n1]
        o_ref[0, 1, kk] = bps[kk][n1:]


def _conv_mid(a, kspec, order, gbb, gtb, *, kb, td):
    b, _, h2, n1, d = a.shape
    return pl.pallas_call(
        functools.partial(_cmid_kernel, kb=kb, n1=n1),
        out_shape=jax.ShapeDtypeStruct(a.shape, F32),
        grid=(h2 // kb, d // td, b),
        in_specs=[
            pl.BlockSpec((1, 2, kb, n1, td), lambda k, t, bi: (bi, 0, k, 0, t)),
            pl.BlockSpec((1, 2, kb, n1, td), lambda k, t, bi: (order, 0, k, 0, t)),
            pl.BlockSpec((kb, 2 * n1, 2 * n1), lambda k, t, bi: (k, 0, 0)),
            pl.BlockSpec((kb, 2 * n1, 2 * n1), lambda k, t, bi: (k, 0, 0)),
        ],
        out_specs=pl.BlockSpec((1, 2, kb, n1, td), lambda k, t, bi: (bi, 0, k, 0, t)),
        compiler_params=_cparams("arbitrary", "arbitrary", "arbitrary"),
        name="hyena_conv_mid",
    )(a, kspec, gbb, gtb)


def _idft1_kernel(bp_ref, m_ref, gate_ref, zz_ref, skip_ref, o_ref, y_scr):
    bp = jnp.concatenate([bp_ref[0, 0], bp_ref[0, 1]], axis=0)
    bt = pltpu.einshape("njd->jnd", bp)
    m = m_ref[...]
    for j in range(IDFT_ROWS):
        y_scr[j] = _dft_dot(m, bt[j])
    y = pltpu.einshape("jhd->hjd", y_scr[...])
    o_ref[0] = (gate_ref[0] * (y + zz_ref[0] * skip_ref[...])).astype(o_ref.dtype)


def _idft1_gate(bp, minvb, u4, zz4, skip, *, td, gate_part, zz_part):
    b, _, h2, n1, d = bp.shape
    nt = d // td
    return pl.pallas_call(
        _idft1_kernel,
        out_shape=jax.ShapeDtypeStruct((b, h2, n1, d), F32),
        grid=(b, n1 // IDFT_ROWS, nt),
        in_specs=[
            pl.BlockSpec((1, 2, h2, IDFT_ROWS, td), lambda bi, g, c: (bi, 0, 0, g, c)),
            pl.BlockSpec(minvb.shape, lambda bi, g, c: (0, 0)),
            pl.BlockSpec((1, h2, IDFT_ROWS, td), lambda bi, g, c: (bi, 0, g, gate_part * nt + c)),
            pl.BlockSpec((1, h2, IDFT_ROWS, td), lambda bi, g, c: (bi, 0, g, zz_part * nt + c)),
            pl.BlockSpec((1, td), lambda bi, g, c: (0, c)),
        ],
        out_specs=pl.BlockSpec((1, h2, IDFT_ROWS, td), lambda bi, g, c: (bi, 0, g, c)),
        scratch_shapes=[pltpu.VMEM((IDFT_ROWS, h2, td), F32)],
        compiler_params=_cparams("arbitrary", "arbitrary", "arbitrary"),
        name="idft_stage1_gate",
    )(bp, minvb, u4, zz4, skip)


def _hyena_features(seq, d):
    t = jnp.linspace(0.0, 1.0, seq, dtype=F32)[:, None]
    w = 2.0 * math.pi * jnp.arange(seq, dtype=F32)[:, None] / seq
    f = jnp.linspace(1e-4, HYENA_BANDS - 1, HYENA_BANDS, dtype=F32)[None, :]
    z = jnp.concatenate([t, jnp.cos(f * w), -jnp.sin(f * w)], axis=-1)
    z = jnp.pad(z, ((0, 0), (0, HYENA_EMB_PAD - HYENA_EMB)))
    deltas = jnp.abs(jnp.linspace(HYENA_MIN_DECAY, HYENA_MAX_DECAY, d, dtype=F32))[None, :]
    return z, deltas


def _hyena_mixer(x, mod, gain, p, *, batch, seq):
    t_rows, d = x.shape
    n, n1, n2 = _dft_sizes(seq)
    h2 = n2 // 2
    u = _up(x, mod, gain, p["w_in"][None], p["b_in"], p["conv_w"], p["conv_b"], layer=0, seq=seq, sh_idx=0,
            sc_idx=1, glu=False, out_dtype=F32)

    z, deltas = _hyena_features(seq, d)
    filt = _filters(z, p["f_w1"], p["f_b1"], p["f_w2"], p["f_b2"], p["f_freq"], p["f_w3"], deltas)

    m1, gb = _dft_tables(seq)
    m1b = m1.astype(BF16)
    minvb = (m1.T * (2.0 / n)).astype(BF16)
    gbb = gb.astype(BF16)
    gtb = jnp.swapaxes(gb, 1, 2).astype(BF16)
    kb = max(1, 256 // n1)
    td = _tile(d, MID_LANES)
    ts = _tile(d, DFT_LANES)
    nts = d // ts

    ncf = filt.shape[1]
    af = _dft1(filt.reshape(1, h2, n1, ncf), m1b, td=ts, n_blocks=ncf // ts, col_map=lambda c: c)
    kspec = _filter_spectrum(af[0], gbb, d=d, kb=kb, td=td)

    u4 = u.reshape(batch, h2, n1, 3 * d)
    zz4 = u4
    for o in range(HYENA_ORDER):
        a = _dft1(zz4, m1b, td=ts, n_blocks=nts, col_map=lambda c: c)
        bp = _conv_mid(a, kspec, o, gbb, gtb, kb=kb, td=td)
        zz4 = _idft1_gate(bp, minvb, u4, zz4, p["skip"][o:o + 1], td=ts, gate_part=1 + o, zz_part=0)
    return zz4.reshape(t_rows, d)


def _trunk(x3, modall, row0, p):
    batch, seq, d = x3.shape
    t_rows = batch * seq
    x = x3.reshape(t_rows, d)
    n_kv = d // HEAD_DIM // GQA_GROUP
    depth = modall.shape[0]
    for i in range(depth):
        mod = modall[i, row0:row0 + batch].reshape(batch, 6, d)
        kind, j = i % 3, i // 3
        g_pre, g_post = p["norm_mix_pre"][i:i + 1], p["norm_mix_post"][i:i + 1]
        if kind == 0:
            x = _pool_layer(x, mod, g_pre, g_post, p["pool_w"][j], p["pool_scale"][j:j + 1], seq=seq)
        elif kind == 1:
            hp = {k: v[j] for k, v in p["hy"].items()}
            zz = _hyena_mixer(x, mod, g_pre, hp, batch=batch, seq=seq)
            x = _down(zz, p["hy"]["w_out"], x, mod, g_post, layer=j, seq=seq, gate_idx=2)
        else:
            cos, sin = _rope_tables(seq)
            gains = jnp.stack([p["at_q_gain"][j], p["at_k_gain"][j]], axis=0)
            q, k, vt = _qkv(x, mod, g_pre, p["at_w_qkv"][j], gains, cos, sin, seq=seq, n_kv=n_kv)
            o = _flash(q, k, vt, batch=batch, seq=seq, n_kv=n_kv)
            x = _down(o, p["at_w_o"], x, mod, g_post, layer=j, seq=seq, gate_idx=2)
        g = _up(x, mod, p["norm_ffn_pre"][i:i + 1], p["ffn_w_up"], None, p["ffn_conv_w"][i],
                p["ffn_conv_b"][i:i + 1], layer=i, seq=seq, sh_idx=3, sc_idx=4, glu=True, out_dtype=BF16)
        x = _down(g, p["ffn_w_down"], x, mod, p["norm_ffn_post"][i:i + 1], layer=i, seq=seq, gate_idx=5)
    return x.reshape(batch, seq, d)


def kernel(x_prompt, x_sample, c_prompt, c_sample, mod_w, mod_b, norm_mix_pre, norm_mix_post, norm_ffn_pre,
           norm_ffn_post, ffn_w_up, ffn_conv_w, ffn_conv_b, ffn_w_down, pool_w, pool_scale, hy_w_in, hy_b_in,
           hy_conv_w, hy_conv_b, hy_f_w1, hy_f_b1, hy_f_w2, hy_f_b2, hy_f_w3, hy_f_freq, hy_skip, hy_w_out,
           at_w_qkv, at_q_gain, at_k_gain, at_w_o):
    nb = c_prompt.shape[0]
    ns = c_sample.shape[0]
    assert nb + ns <= MOD_ROWS
    c8 = jnp.concatenate([c_prompt, c_sample, jnp.zeros((MOD_ROWS - nb - ns, c_prompt.shape[1]), F32)], axis=0)
    modall = _modulation(c8, mod_w, mod_b)
    row = lambda v: v[:, None, :]
    p = dict(
        norm_mix_pre=norm_mix_pre, norm_mix_post=norm_mix_post, norm_ffn_pre=norm_ffn_pre,
        norm_ffn_post=norm_ffn_post,
        ffn_w_up=ffn_w_up.astype(BF16), ffn_conv_w=ffn_conv_w, ffn_conv_b=ffn_conv_b,
        ffn_w_down=ffn_w_down.astype(BF16),
        pool_w=pool_w.astype(BF16), pool_scale=pool_scale,
        hy=dict(
            w_in=hy_w_in.astype(BF16), b_in=row(hy_b_in), conv_w=hy_conv_w, conv_b=row(hy_conv_b),
            f_w1=jnp.pad(hy_f_w1, ((0, 0), (0, HYENA_EMB_PAD - HYENA_EMB), (0, 0))), f_b1=row(hy_f_b1),
            f_w2=hy_f_w2, f_b2=row(hy_f_b2), f_w3=hy_f_w3, f_freq=row(hy_f_freq), skip=hy_skip,
            w_out=hy_w_out.astype(BF16)),
        at_w_qkv=at_w_qkv.astype(BF16), at_q_gain=at_q_gain, at_k_gain=at_k_gain, at_w_o=at_w_o.astype(BF16),
    )
    y_prompt = _trunk(x_prompt, modall, 0, p)
    y_sample = _trunk(x_sample, modall, nb, p)
    return (y_prompt, y_sample)
```

```python
import functools
import math

import jax
import jax.numpy as jnp
from jax import lax
from jax.experimental import pallas as pl
from jax.experimental.pallas import tpu as pltpu

EPS = 1e-6
HEAD_DIM = 128
GQA_GROUP = 4
GRID_W = 64
ROPE_THETA = 10000.0
POOL_WINDOWS = (2, 4, 8, 16)
HYENA_ORDER = 2
HYENA_EMB = 33
HYENA_BANDS = (HYENA_EMB - 1) // 2
HYENA_EMB_PAD = 64
HYENA_FAST_DECAY = 0.3
HYENA_SLOW_DECAY = 1.5
HYENA_TARGET = 1e-2
HYENA_MOD_SHIFT = 0.05
HYENA_MIN_DECAY = math.log(HYENA_TARGET) / HYENA_SLOW_DECAY
HYENA_MAX_DECAY = math.log(HYENA_TARGET) / HYENA_FAST_DECAY

HALO = 16
MOD_ROWS = 8
DFT_N2_MAX = 512
DFT_N1_MIN = 16
DFT_ROWS = 16
IDFT_ROWS = 8
LANES = 128
UP_SUB = 512
FLASH_TK = 512
ONES_ROWS = 16
LOG2E = 1.4426950408889634
VMEM_LIMIT = 58 * 1024 * 1024

UP_TM = 1024
DOWN_TM = 512
DOWN_TK_MAX = 2816
POOL_TM = 512
QKV_TM = 512
FLASH_TQ = 512
MOD_TN = 1024
FILT_ROWS = 256
MID_LANES = 2048
DFT_LANES = 256

F32 = jnp.float32
BF16 = jnp.bfloat16


def _cparams(*sem):
    return pltpu.CompilerParams(dimension_semantics=sem, vmem_limit_bytes=VMEM_LIMIT)


def _tile(dim, pref):
    t = min(dim, pref)
    while dim % t:
        t //= 2
    return t


def _lane_tile(dim, cap):
    units = dim // LANES
    best = max(u for u in range(1, units + 1) if units % u == 0 and u * LANES <= max(cap, LANES))
    return best * LANES


def _split3(x):
    hi = x.astype(BF16)
    lo = (x - hi.astype(F32)).astype(BF16)
    return hi, lo


def _dot3(a, b):
    ah, al = _split3(a)
    bh, bl = _split3(b)
    d = functools.partial(jnp.dot, preferred_element_type=F32)
    return d(ah, bh) + (d(ah, bl) + d(al, bh))


def _dft_dot(m, x):
    return jnp.dot(m, x.astype(BF16), preferred_element_type=F32)


def _modnorm(xv, gain, scale1p, shift):
    ms = jnp.mean(xv * xv, axis=-1, keepdims=True)
    return xv * lax.rsqrt(ms + EPS) * gain * scale1p + shift


def _halo_maps(tm, t_rows):
    r = tm // HALO
    last = t_rows // HALO - 1
    prev = lambda i, *_: (jnp.maximum(i * r - 1, 0), 0)
    nxt = lambda i, *_: (jnp.minimum((i + 1) * r, last), 0)
    return prev, nxt


def _mod_kernel(c_ref, w_ref, b_ref, o_ref):
    c = c_ref[...]
    a = (c / (1.0 + jnp.exp(-c))).astype(BF16)
    o_ref[0] = jnp.dot(a, w_ref[0].astype(BF16), preferred_element_type=F32) + b_ref[0]


def _modulation(c8, mod_w, mod_b):
    depth, d, n = mod_w.shape
    tn = _tile(n, MOD_TN)
    return pl.pallas_call(
        _mod_kernel,
        out_shape=jax.ShapeDtypeStruct((depth, MOD_ROWS, n), F32),
        grid=(depth, n // tn),
        in_specs=[
            pl.BlockSpec((MOD_ROWS, d), lambda l, j: (0, 0)),
            pl.BlockSpec((1, d, tn), lambda l, j: (l, 0, j)),
            pl.BlockSpec((1, 1, tn), lambda l, j: (l, 0, j)),
        ],
        out_specs=pl.BlockSpec((1, MOD_ROWS, tn), lambda l, j: (l, 0, j)),
        compiler_params=_cparams("arbitrary", "arbitrary"),
        name="modulation",
    )(c8, mod_w, mod_b.reshape(depth, 1, n))


def _fill_h(h_scr, x_ref, xp_ref, xn_ref, mod_ref, g_ref, sh_idx, sc_idx, tm):
    m = mod_ref[0]
    shift = m[sh_idx:sh_idx + 1]
    scale1p = 1.0 + m[sc_idx:sc_idx + 1]
    gain = g_ref[...]
    h_scr[0:HALO] = _modnorm(xp_ref[...], gain, scale1p, shift).astype(h_scr.dtype)
    h_scr[HALO:HALO + tm] = _modnorm(x_ref[...], gain, scale1p, shift).astype(h_scr.dtype)
    h_scr[HALO + tm:] = _modnorm(xn_ref[...], gain, scale1p, shift).astype(h_scr.dtype)


def _edge_keep(tps):
    i = pl.program_id(0) % tps
    return (i != 0).astype(F32), (i != tps - 1).astype(F32)


def _conv3_rows(a, keep_first, keep_last, cw, cb, tm):
    rows = tm + 2 * HALO
    a = jnp.concatenate([a[0:HALO] * keep_first, a[HALO:HALO + tm], a[HALO + tm:] * keep_last], axis=0)
    prev = pltpu.roll(a, 1, 0)
    nxt = pltpu.roll(a, rows - 1, 0)
    c = prev * cw[0:1] + a * cw[1:2] + nxt * cw[2:3] + cb
    return c[HALO:HALO + tm]


def _gelu_tanh(x):
    return 0.5 * x * (1.0 + jnp.tanh(0.7978845608028654 * (x + 0.044715 * (x * x * x))))


def _up_kernel(*refs, sh_idx, sc_idx, tm, tps, glu, has_bias, n_chunks):
    x_ref, xp_ref, xn_ref, mod_ref, g_ref, w_ref = refs[:6]
    k = 6
    wb_ref = b_ref = None
    if glu:
        wb_ref = refs[k]
        k += 1
    if has_bias:
        b_ref = refs[k]
        k += 1
    cw_ref, cb_ref, o_ref, h_scr, hn_scr = refs[k:k + 5]
    i = pl.program_id(0)
    j = pl.program_id(1)

    @pl.when(jnp.logical_and(j == 0, i == 0))
    def _():
        _fill_h(h_scr, x_ref, xp_ref, xn_ref, mod_ref, g_ref, sh_idx, sc_idx, tm)

    @pl.when(jnp.logical_and(j == 0, i > 0))
    def _():
        h_scr[...] = hn_scr[...]

    m = mod_ref[0]
    shift = m[sh_idx:sh_idx + 1]
    scale1p = 1.0 + m[sc_idx:sc_idx + 1]
    gain = g_ref[...]

    @pl.when(j == 1)
    def _():
        hn_scr[0:HALO] = _modnorm(xp_ref[...], gain, scale1p, shift).astype(hn_scr.dtype)
        hn_scr[HALO + tm:] = _modnorm(xn_ref[...], gain, scale1p, shift).astype(hn_scr.dtype)

    rc = tm // n_chunks
    chunk = jnp.clip(j - 1, 0, n_chunks - 1)
    r0 = pl.multiple_of(chunk * rc, HALO)
    hn_scr[pl.ds(pl.multiple_of(HALO + r0, HALO), rc), :] = _modnorm(
        x_ref[pl.ds(r0, rc), :], gain, scale1p, shift).astype(hn_scr.dtype)

    keep_first, keep_last = _edge_keep(tps)
    for s in range(o_ref.shape[1] // UP_SUB):
        sl = slice(s * UP_SUB, (s + 1) * UP_SUB)
        a = jnp.dot(h_scr[...], w_ref[:, sl], preferred_element_type=F32)
        if has_bias:
            a = a + b_ref[:, sl]
        c = _conv3_rows(a, keep_first, keep_last, cw_ref[:, sl], cb_ref[:, sl], tm)
        if glu:
            b = jnp.dot(h_scr[HALO:HALO + tm], wb_ref[:, sl], preferred_element_type=F32)
            c = _gelu_tanh(c) * b
        o_ref[:, sl] = c.astype(o_ref.dtype)


def _up(x, mod, gain, w, bias, cw, cb, *, layer, seq, sh_idx, sc_idx, glu, out_dtype):
    t_rows, d = x.shape
    n_out = cw.shape[-1]
    tm = _tile(seq, UP_TM)
    tn = _tile(n_out, UP_SUB if glu else 2 * UP_SUB)
    assert tn % UP_SUB == 0
    tps = seq // tm
    nj = n_out // tn
    n_tiles = t_rows // tm
    assert nj >= 2
    n_chunks = 1
    while 2 * n_chunks <= nj - 1 and tm % (2 * n_chunks * HALO) == 0:
        n_chunks *= 2
    r = tm // HALO

    def ahead(i, j):
        return jnp.minimum(i + jnp.where(jnp.logical_and(i == 0, j == 0), 0, 1), n_tiles - 1)

    in_specs = [
        pl.BlockSpec((tm, d), lambda i, j: (ahead(i, j), 0)),
        pl.BlockSpec((HALO, d), lambda i, j: (jnp.maximum(ahead(i, j) * r - 1, 0), 0)),
        pl.BlockSpec((HALO, d), lambda i, j: (jnp.minimum((ahead(i, j) + 1) * r, t_rows // HALO - 1), 0)),
        pl.BlockSpec((1, 6, d), lambda i, j: (ahead(i, j) // tps, 0, 0)),
        pl.BlockSpec((1, d), lambda i, j: (0, 0)),
        pl.BlockSpec((None, d, tn), lambda i, j: (layer, 0, j)),
    ]
    args = [x, x, x, mod, gain, w]
    if glu:
        in_specs.append(pl.BlockSpec((None, d, tn), lambda i, j: (layer, 0, j + nj)))
        args.append(w)
    if bias is not None:
        in_specs.append(pl.BlockSpec((1, tn), lambda i, j: (0, j)))
        args.append(bias)
    in_specs += [pl.BlockSpec((3, tn), lambda i, j: (0, j)), pl.BlockSpec((1, tn), lambda i, j: (0, j))]
    args += [cw, cb]
    return pl.pallas_call(
        functools.partial(_up_kernel, sh_idx=sh_idx, sc_idx=sc_idx, tm=tm, tps=tps, glu=glu,
                          has_bias=bias is not None, n_chunks=n_chunks),
        out_shape=jax.ShapeDtypeStruct((t_rows, n_out), out_dtype),
        grid=(n_tiles, nj),
        in_specs=in_specs,
        out_specs=pl.BlockSpec((tm, tn), lambda i, j: (i, j)),
        scratch_shapes=[pltpu.VMEM((tm + 2 * HALO, d), BF16), pltpu.VMEM((tm + 2 * HALO, d), BF16)],
        compiler_params=_cparams("arbitrary", "arbitrary"),
        name="up_glu" if glu else "up_conv",
    )(*args)


def _down_kernel(g_ref, w_ref, x_ref, mod_ref, gain_ref, o_ref, *scratch, gate_idx, nk):
    k = pl.program_id(1)
    acc_ref = scratch[0] if nk > 1 else None
    part = jnp.dot(g_ref[...].astype(w_ref.dtype), w_ref[...], preferred_element_type=F32)

    if nk > 1:
        @pl.when(k == 0)
        def _():
            acc_ref[...] = part

    if nk > 2:
        @pl.when(jnp.logical_and(k > 0, k < nk - 1))
        def _():
            acc_ref[...] += part

    @pl.when(k == nk - 1)
    def _():
        f = acc_ref[...] + part if nk > 1 else part
        ms = jnp.mean(f * f, axis=-1, keepdims=True)
        y = f * lax.rsqrt(ms + EPS) * gain_ref[...]
        gate = mod_ref[0][gate_idx:gate_idx + 1]
        o_ref[...] = x_ref[...] + gate * y


def _down(g, w, x, mod, gain, *, layer, seq, gate_idx):
    t_rows, kdim = g.shape
    d = w.shape[2]
    tm = _tile(seq, DOWN_TM)
    tk = _lane_tile(kdim, DOWN_TK_MAX)
    tps = seq // tm
    nk = kdim // tk
    return pl.pallas_call(
        functools.partial(_down_kernel, gate_idx=gate_idx, nk=nk),
        out_shape=jax.ShapeDtypeStruct((t_rows, d), F32),
        grid=(t_rows // tm, nk),
        in_specs=[
            pl.BlockSpec((tm, tk), lambda i, k: (i, k)),
            pl.BlockSpec((None, tk, d), lambda i, k: (layer, k, 0)),
            pl.BlockSpec((tm, d), lambda i, k: (i, 0)),
            pl.BlockSpec((1, 6, d), lambda i, k: (i // tps, 0, 0)),
            pl.BlockSpec((1, d), lambda i, k: (0, 0)),
        ],
        out_specs=pl.BlockSpec((tm, d), lambda i, k: (i, 0)),
        scratch_shapes=[pltpu.VMEM((tm, d), F32)] if nk > 1 else [],
        compiler_params=_cparams("arbitrary", "arbitrary"),
        name="down",
    )(g, w, x, mod, gain)


def _pool_kernel(x_ref, xp_ref, xn_ref, mod_ref, gpre_ref, gpost_ref, pw_ref, ps_ref, o_ref, h_scr,
                 *, tm, tps, seq):
    _fill_h(h_scr, x_ref, xp_ref, xn_ref, mod_ref, gpre_ref, 0, 1, tm)
    keep_first, keep_last = _edge_keep(tps)
    rows = tm + 2 * HALO
    d = x_ref.shape[1]
    cg = d // len(POOL_WINDOWS)
    pos = (pl.program_id(0) % tps) * tm + lax.broadcasted_iota(jnp.int32, (tm, 1), 0)
    ys = []
    ssq = jnp.zeros((tm, 1), F32)
    for g, win in enumerate(POOL_WINDOWS):
        half = win // 2
        sl = slice(g * cg, (g + 1) * cg)
        hg = jnp.concatenate([h_scr[0:HALO, sl] * keep_first, h_scr[HALO:HALO + tm, sl],
                              h_scr[HALO + tm:, sl] * keep_last], axis=0)
        p = hg
        s = 1
        while s < win:
            p = p + pltpu.roll(p, s, 0)
            s *= 2
        if half > 1:
            p = pltpu.roll(p, rows - (half - 1), 0)
        lo = jnp.maximum(pos - half, 0)
        hi = jnp.minimum(pos + (half - 1), seq - 1)
        inv_cnt = 1.0 / (hi - lo + 1).astype(F32)
        pooled = p[HALO:HALO + tm] * inv_cnt - hg[HALO:HALO + tm]
        y = jnp.dot(pooled.astype(BF16), pw_ref[g], preferred_element_type=F32) * ps_ref[:, sl]
        ssq = ssq + jnp.sum(y * y, axis=-1, keepdims=True)
        ys.append(y)
    inv = lax.rsqrt(ssq / d + EPS)
    gate = mod_ref[0][2:3]
    for g in range(len(POOL_WINDOWS)):
        sl = slice(g * cg, (g + 1) * cg)
        o_ref[:, sl] = x_ref[:, sl] + gate[:, sl] * (ys[g] * inv * gpost_ref[:, sl])


def _pool_layer(x, mod, gpre, gpost, pw, ps, *, seq):
    t_rows, d = x.shape
    tm = _tile(seq, POOL_TM)
    tps = seq // tm
    prev, nxt = _halo_maps(tm, t_rows)
    return pl.pallas_call(
        functools.partial(_pool_kernel, tm=tm, tps=tps, seq=seq),
        out_shape=jax.ShapeDtypeStruct((t_rows, d), F32),
        grid=(t_rows // tm,),
        in_specs=[
            pl.BlockSpec((tm, d), lambda i: (i, 0)),
            pl.BlockSpec((HALO, d), prev),
            pl.BlockSpec((HALO, d), nxt),
            pl.BlockSpec((1, 6, d), lambda i: (i // tps, 0, 0)),
            pl.BlockSpec((1, d), lambda i: (0, 0)),
            pl.BlockSpec((1, d), lambda i: (0, 0)),
            pl.BlockSpec(pw.shape, lambda i: (0, 0, 0)),
            pl.BlockSpec((1, d), lambda i: (0, 0)),
        ],
        out_specs=pl.BlockSpec((tm, d), lambda i: (i, 0)),
        scratch_shapes=[pltpu.VMEM((tm + 2 * HALO, d), F32)],
        compiler_params=_cparams("arbitrary"),
        name="pool_layer",
    )(x, x, x, mod, gpre, gpost, pw, ps)


def _rope_tables(seq):
    pos = jnp.arange(seq, dtype=jnp.int32)
    row = (pos // GRID_W).astype(F32)[:, None]
    col = (pos % GRID_W).astype(F32)[:, None]
    axis_dim = HEAD_DIM // 2
    inv = ROPE_THETA ** (-jnp.arange(0, axis_dim, 2, dtype=F32) / axis_dim)
    ar, ac = row * inv, col * inv
    cos = jnp.concatenate([jnp.cos(ar), jnp.cos(ar), jnp.cos(ac), jnp.cos(ac)], axis=-1)
    sin = jnp.concatenate([-jnp.sin(ar), jnp.sin(ar), -jnp.sin(ac), jnp.sin(ac)], axis=-1)
    return cos, sin


def _qkv_kernel(x_ref, mod_ref, g_ref, w0_ref, w1_ref, hg_ref, cos_ref, sin_ref, q_ref, k_ref, vt_ref, h_scr, *,
                n_q_steps):
    j = pl.program_id(1)
    tn = k_ref.shape[1]

    @pl.when(j == 0)
    def _():
        m = mod_ref[0]
        h_scr[...] = _modnorm(x_ref[...], g_ref[...], 1.0 + m[1:2], m[0:1]).astype(BF16)

    def proj(t):
        return jnp.dot(h_scr[...], (w0_ref, w1_ref)[t][...], preferred_element_type=F32)

    def norm_rope(a, gain, out_scale, o_ref, col0):
        cos = cos_ref[...]
        sin = sin_ref[...]
        quarter = HEAD_DIM // 4
        lane = lax.broadcasted_iota(jnp.int32, (1, HEAD_DIM), 1)
        low = (lane % (2 * quarter)) < quarter
        for h in range(tn // HEAD_DIM):
            v = a[:, h * HEAD_DIM:(h + 1) * HEAD_DIM]
            ms = jnp.mean(v * v, axis=-1, keepdims=True)
            v = v * lax.rsqrt(ms + EPS) * gain
            swapped = jnp.where(low, pltpu.roll(v, HEAD_DIM - quarter, 1), pltpu.roll(v, quarter, 1))
            o_ref[:, col0 + h * HEAD_DIM:col0 + (h + 1) * HEAD_DIM] = (
                (v * cos + swapped * sin) * out_scale).astype(o_ref.dtype)

    @pl.when(j < n_q_steps)
    def _():
        for t in range(2):
            norm_rope(proj(t), hg_ref[0:1], HEAD_DIM ** -0.5 * LOG2E, q_ref, t * tn)

    @pl.when(j == n_q_steps)
    def _():
        norm_rope(proj(0), hg_ref[1:2], 1.0, k_ref, 0)
        v = proj(1)
        for c in range(vt_ref.shape[0]):
            vt_ref[c] = v[c * FLASH_TK:(c + 1) * FLASH_TK].T.astype(vt_ref.dtype)


def _qkv(x, mod, gain, w, head_gains, cos, sin, *, seq, n_kv):
    t_rows, d = x.shape
    n_out = w.shape[1]
    tm = _tile(seq, QKV_TM)
    tn = HEAD_DIM * n_kv
    tps = seq // tm
    n_q_steps = d // (2 * tn)
    assert tm % FLASH_TK == 0 and n_out == d + 2 * tn and d % (2 * tn) == 0
    return pl.pallas_call(
        functools.partial(_qkv_kernel, n_q_steps=n_q_steps),
        out_shape=(jax.ShapeDtypeStruct((t_rows, d), BF16), jax.ShapeDtypeStruct((t_rows, tn), BF16),
                   jax.ShapeDtypeStruct((t_rows // FLASH_TK, tn, FLASH_TK), BF16)),
        grid=(t_rows // tm, n_q_steps + 1),
        in_specs=[
            pl.BlockSpec((tm, d), lambda i, j: (i, 0)),
            pl.BlockSpec((1, 6, d), lambda i, j: (i // tps, 0, 0)),
            pl.BlockSpec((1, d), lambda i, j: (0, 0)),
            pl.BlockSpec((d, tn), lambda i, j: (0, 2 * j)),
            pl.BlockSpec((d, tn), lambda i, j: (0, 2 * j + 1)),
            pl.BlockSpec((2, HEAD_DIM), lambda i, j: (0, 0)),
            pl.BlockSpec((tm, HEAD_DIM), lambda i, j: (i % tps, 0)),
            pl.BlockSpec((tm, HEAD_DIM), lambda i, j: (i % tps, 0)),
        ],
        out_specs=(
            pl.BlockSpec((tm, 2 * tn), lambda i, j: (i, jnp.minimum(j, n_q_steps - 1))),
            pl.BlockSpec((tm, tn), lambda i, j: (i, 0)),
            pl.BlockSpec((tm // FLASH_TK, tn, FLASH_TK), lambda i, j: (i, 0, 0)),
        ),
        scratch_shapes=[pltpu.VMEM((tm, d), BF16)],
        compiler_params=_cparams("arbitrary", "arbitrary"),
        name="qkv_rope",
    )(x, mod, gain, w, w, head_gains, cos, sin)


def _flash_kernel(q_ref, k_ref, vt_ref, o_ref, acc_ref, s_ref, *, tq, nk):
    q = q_ref[...]
    qs = jnp.concatenate([q[:, h * HEAD_DIM:(h + 1) * HEAD_DIM] for h in range(GQA_GROUP)], axis=0)
    ones = jnp.ones((ONES_ROWS, FLASH_TK), BF16)
    acc_ref[...] = jnp.zeros_like(acc_ref)

    def scores(kk, slot):
        start = pl.multiple_of(kk * FLASH_TK, FLASH_TK)
        s_ref[slot] = lax.dot_general(k_ref[pl.ds(start, FLASH_TK), :], qs, (((1,), (1,)), ((), ())),
                                      preferred_element_type=F32)

    def accumulate(kk, slot, m):
        st = s_ref[slot]
        m_new = jnp.maximum(m, jnp.max(st, axis=0, keepdims=True))
        alpha = jnp.exp2(m - m_new)
        pt = jnp.exp2(st - m_new).astype(BF16)
        va = jnp.concatenate([vt_ref[kk], ones], axis=0)
        acc_ref[...] = alpha * acc_ref[...] + jnp.dot(va, pt, preferred_element_type=F32)
        return m_new

    scores(0, 0)
    unroll = 4 if nk % 4 == 0 else 2

    def group(i, m):
        k0 = unroll * i
        for u in range(unroll):
            nxt = k0 + u + 1
            scores(nxt if u + 1 < unroll else jnp.minimum(nxt, nk - 1), (u + 1) % 2)
            m = accumulate(k0 + u, u % 2, m)
        return m

    lax.fori_loop(0, nk // unroll, group, jnp.full((1, GQA_GROUP * tq), -jnp.inf, F32))
    acc = acc_ref[...]
    ot = acc[:HEAD_DIM] / acc[HEAD_DIM:HEAD_DIM + 1]
    for h in range(GQA_GROUP):
        o_ref[:, h * HEAD_DIM:(h + 1) * HEAD_DIM] = ot[:, h * tq:(h + 1) * tq].T.astype(o_ref.dtype)


def _flash(q, k, vt, *, batch, seq, n_kv):
    t_rows, d = q.shape
    tq = _tile(seq, FLASH_TQ)
    nqb = seq // tq
    nk = seq // FLASH_TK
    assert nk % 2 == 0
    gw = GQA_GROUP * HEAD_DIM
    return pl.pallas_call(
        functools.partial(_flash_kernel, tq=tq, nk=nk),
        out_shape=jax.ShapeDtypeStruct((t_rows, d), BF16),
        grid=(batch, n_kv, nqb),
        in_specs=[
            pl.BlockSpec((tq, gw), lambda b, kv, qi: (b * nqb + qi, kv)),
            pl.BlockSpec((seq, HEAD_DIM), lambda b, kv, qi: (b, kv)),
            pl.BlockSpec((nk, HEAD_DIM, FLASH_TK), lambda b, kv, qi: (b, kv, 0)),
        ],
        out_specs=pl.BlockSpec((tq, gw), lambda b, kv, qi: (b * nqb + qi, kv)),
        scratch_shapes=[pltpu.VMEM((HEAD_DIM + ONES_ROWS, GQA_GROUP * tq), F32),
                        pltpu.VMEM((2, FLASH_TK, GQA_GROUP * tq), F32)],
        compiler_params=_cparams("arbitrary", "arbitrary", "arbitrary"),
        name="flash_attention",
    )(q, k, vt)


def _dft_sizes(seq):
    n = 2 * seq
    n2 = DFT_N2_MAX
    while n // n2 < DFT_N1_MIN:
        n2 //= 2
    return n, n // n2, n2


def _dft_tables(seq):
    n, n1, n2 = _dft_sizes(seq)
    h2 = n2 // 2
    k2 = jnp.arange(h2, dtype=jnp.int32)
    m2 = jnp.arange(h2, dtype=jnp.int32)
    idx = (m2[None, :] * (2 * k2[:, None] + 1)) % (2 * n2)
    th = idx.astype(F32) * (2.0 * math.pi / (2 * n2))
    m1 = jnp.concatenate([jnp.cos(th), -jnp.sin(th)], axis=0)
    a1 = jnp.arange(n1, dtype=jnp.int32)
    k1 = jnp.arange(n1, dtype=jnp.int32)
    freq = 2 * n2 * k1[None, :, None] + 2 * k2[:, None, None] + 1
    idx = (a1[None, None, :] * freq) % (2 * n)
    ph = idx.astype(F32) * (2.0 * math.pi / (2 * n))
    c, s = jnp.cos(ph), jnp.sin(ph)
    gb = jnp.concatenate([jnp.concatenate([c, s], axis=2), jnp.concatenate([-s, c], axis=2)], axis=1)
    return m1, gb


def _filt_kernel(z_ref, w1_ref, b1_ref, w2_ref, b2_ref, fr_ref, w3_ref, dl_ref, o_ref):
    z = z_ref[...]
    fr = fr_ref[...]
    a = jnp.sin(fr * (_dot3(z, w1_ref[...]) + b1_ref[...]))
    a = jnp.sin(fr * (_dot3(a, w2_ref[...]) + b2_ref[...]))
    decay = jnp.exp(-z[:, 0:1] * dl_ref[...]) + HYENA_MOD_SHIFT
    d = decay.shape[1]
    row = pl.program_id(0) * z.shape[0] + lax.broadcasted_iota(jnp.int32, (z.shape[0], 1), 0)
    for c in range(w3_ref.shape[1] // d):
        f = _dot3(a, w3_ref[:, c * d:(c + 1) * d]) * decay
        o_ref[:, c * d:(c + 1) * d] = jnp.where(row == 0, 0.0, f) if c % 2 == 1 else f


def _filters(z, w1, b1, w2, b2, fr, w3, deltas):
    seq = z.shape[0]
    d = deltas.shape[1]
    tl = _tile(seq, FILT_ROWS)
    hid = w2.shape[0]
    const = lambda i: (0, 0)
    return pl.pallas_call(
        _filt_kernel,
        out_shape=jax.ShapeDtypeStruct((seq, w3.shape[1]), F32),
        grid=(seq // tl,),
        in_specs=[
            pl.BlockSpec((tl, HYENA_EMB_PAD), lambda i: (i, 0)),
            pl.BlockSpec((HYENA_EMB_PAD, hid), const),
            pl.BlockSpec((1, hid), const),
            pl.BlockSpec((hid, hid), const),
            pl.BlockSpec((1, hid), const),
            pl.BlockSpec((1, hid), const),
            pl.BlockSpec(w3.shape, const),
            pl.BlockSpec((1, d), const),
        ],
        out_specs=pl.BlockSpec((tl, w3.shape[1]), lambda i: (i, 0)),
        compiler_params=_cparams("arbitrary"),
        name="hyena_filters",
    )(z, w1, b1, w2, b2, fr, w3, deltas)


def _dft1_kernel(x_ref, m_ref, o_ref, y_scr):
    xt = pltpu.einshape("hjd->jhd", x_ref[0])
    m = m_ref[...]
    for j in range(DFT_ROWS):
        y_scr[j] = _dft_dot(m, xt[j])
    yt = pltpu.einshape("jnd->njd", y_scr[...]).astype(o_ref.dtype)
    h2 = yt.shape[0] // 2
    o_ref[0, 0] = yt[:h2]
    o_ref[0, 1] = yt[h2:]


def _dft1(x4, m1b, *, td, n_blocks, col_map):
    b, h2, n1, _ = x4.shape
    n2 = m1b.shape[0]
    return pl.pallas_call(
        _dft1_kernel,
        out_shape=jax.ShapeDtypeStruct((b, 2, h2, n1, n_blocks * td), BF16),
        grid=(b, n1 // DFT_ROWS, n_blocks),
        in_specs=[
            pl.BlockSpec((1, h2, DFT_ROWS, td), lambda bi, g, c: (bi, 0, g, col_map(c))),
            pl.BlockSpec(m1b.shape, lambda bi, g, c: (0, 0)),
        ],
        out_specs=pl.BlockSpec((1, 2, h2, DFT_ROWS, td), lambda bi, g, c: (bi, 0, 0, g, c)),
        scratch_shapes=[pltpu.VMEM((DFT_ROWS, n2, td), F32)],
        compiler_params=_cparams("arbitrary", "arbitrary", "arbitrary"),
        name="dft_stage1",
    )(x4, m1b)


def _fspec_kernel(af_ref, ab_ref, g_ref, o_ref, *, kb, n1):
    xf = [_dft_dot(g_ref[kk], jnp.concatenate([af_ref[0, kk], af_ref[1, kk]], axis=0)) for kk in range(kb)]
    xb = [_dft_dot(g_ref[kk], jnp.concatenate([ab_ref[0, kk], ab_ref[1, kk]], axis=0)) for kk in range(kb)]
    for kk in range(kb):
        o_ref[0, 0, kk] = (xf[kk][:n1] + xb[kk][:n1]).astype(o_ref.dtype)
        o_ref[0, 1, kk] = (xf[kk][n1:] - xb[kk][n1:]).astype(o_ref.dtype)


def _filter_spectrum(af, gbb, *, d, kb, td):
    _, h2, n1, w = af.shape
    ndt = d // td
    norder = w // (2 * d)
    return pl.pallas_call(
        functools.partial(_fspec_kernel, kb=kb, n1=n1),
        out_shape=jax.ShapeDtypeStruct((norder, 2, h2, n1, d), BF16),
        grid=(norder, h2 // kb, ndt),
        in_specs=[
            pl.BlockSpec((2, kb, n1, td), lambda o, k, t: (0, k, 0, (2 * o) * ndt + t)),
            pl.BlockSpec((2, kb, n1, td), lambda o, k, t: (0, k, 0, (2 * o + 1) * ndt + t)),
            pl.BlockSpec((kb, 2 * n1, 2 * n1), lambda o, k, t: (k, 0, 0)),
        ],
        out_specs=pl.BlockSpec((1, 2, kb, n1, td), lambda o, k, t: (o, 0, k, 0, t)),
        compiler_params=_cparams("arbitrary", "arbitrary", "arbitrary"),
        name="hyena_filter_spectrum",
    )(af, af, gbb)


def _cmid_kernel(a_ref, ks_ref, g_ref, gt_ref, o_ref, *, kb, n1):
    xs = [_dft_dot(g_ref[kk], jnp.concatenate([a_ref[0, 0, kk], a_ref[0, 1, kk]], axis=0)) for kk in range(kb)]
    ys = []
    for kk in range(kb):
        xr, xi = xs[kk][:n1], xs[kk][n1:]
        kr, ki = ks_ref[0, 0, kk], ks_ref[0, 1, kk]
        ys.append(jnp.concatenate([xr * kr - xi * ki, xr * ki + xi * kr], axis=0))
    bps = [_dft_dot(gt_ref[kk], ys[kk]) for kk in range(kb)]
    for kk in range(kb):
        o_ref[0, 0, kk] = bps[kk][:n1]
        o_ref[0, 1, kk] = bps[kk][n1:]


def _conv_mid(a, kspec, order, gbb, gtb, *, kb, td):
    b, _, h2, n1, d = a.shape
    return pl.pallas_call(
        functools.partial(_cmid_kernel, kb=kb, n1=n1),
        out_shape=jax.ShapeDtypeStruct(a.shape, F32),
        grid=(h2 // kb, d // td, b),
        in_specs=[
            pl.BlockSpec((1, 2, kb, n1, td), lambda k, t, bi: (bi, 0, k, 0, t)),
            pl.BlockSpec((1, 2, kb, n1, td), lambda k, t, bi: (order, 0, k, 0, t)),
            pl.BlockSpec((kb, 2 * n1, 2 * n1), lambda k, t, bi: (k, 0, 0)),
            pl.BlockSpec((kb, 2 * n1, 2 * n1), lambda k, t, bi: (k, 0, 0)),
        ],
        out_specs=pl.BlockSpec((1, 2, kb, n1, td), lambda k, t, bi: (bi, 0, k, 0, t)),
        compiler_params=_cparams("arbitrary", "arbitrary", "arbitrary"),
        name="hyena_conv_mid",
    )(a, kspec, gbb, gtb)


def _idft1_kernel(bp_ref, m_ref, gate_ref, zz_ref, skip_ref, o_ref, y_scr):
    bp = jnp.concatenate([bp_ref[0, 0], bp_ref[0, 1]], axis=0)
    bt = pltpu.einshape("njd->jnd", bp)
    m = m_ref[...]
    for j in range(IDFT_ROWS):
        y_scr[j] = _dft_dot(m, bt[j])
    y = pltpu.einshape("jhd->hjd", y_scr[...])
    o_ref[0] = (gate_ref[0] * (y + zz_ref[0] * skip_ref[...])).astype(o_ref.dtype)


def _idft1_gate(bp, minvb, u4, zz4, skip, *, td, gate_part, zz_part):
    b, _, h2, n1, d = bp.shape
    nt = d // td
    return pl.pallas_call(
        _idft1_kernel,
        out_shape=jax.ShapeDtypeStruct((b, h2, n1, d), F32),
        grid=(b, n1 // IDFT_ROWS, nt),
        in_specs=[
            pl.BlockSpec((1, 2, h2, IDFT_ROWS, td), lambda bi, g, c: (bi, 0, 0, g, c)),
            pl.BlockSpec(minvb.shape, lambda bi, g, c: (0, 0)),
            pl.BlockSpec((1, h2, IDFT_ROWS, td), lambda bi, g, c: (bi, 0, g, gate_part * nt + c)),
            pl.BlockSpec((1, h2, IDFT_ROWS, td), lambda bi, g, c: (bi, 0, g, zz_part * nt + c)),
            pl.BlockSpec((1, td), lambda bi, g, c: (0, c)),
        ],
        out_specs=pl.BlockSpec((1, h2, IDFT_ROWS, td), lambda bi, g, c: (bi, 0, g, c)),
        scratch_shapes=[pltpu.VMEM((IDFT_ROWS, h2, td), F32)],
        compiler_params=_cparams("arbitrary", "arbitrary", "arbitrary"),
        name="idft_stage1_gate",
    )(bp, minvb, u4, zz4, skip)


def _hyena_features(seq, d):
    t = jnp.linspace(0.0, 1.0, seq, dtype=F32)[:, None]
    w = 2.0 * math.pi * jnp.arange(seq, dtype=F32)[:, None] / seq
    f = jnp.linspace(1e-4, HYENA_BANDS - 1, HYENA_BANDS, dtype=F32)[None, :]
    z = jnp.concatenate([t, jnp.cos(f * w), -jnp.sin(f * w)], axis=-1)
    z = jnp.pad(z, ((0, 0), (0, HYENA_EMB_PAD - HYENA_EMB)))
    deltas = jnp.abs(jnp.linspace(HYENA_MIN_DECAY, HYENA_MAX_DECAY, d, dtype=F32))[None, :]
    return z, deltas


def _hyena_mixer(x, mod, gain, p, *, batch, seq):
    t_rows, d = x.shape
    n, n1, n2 = _dft_sizes(seq)
    h2 = n2 // 2
    u = _up(x, mod, gain, p["w_in"][None], p["b_in"], p["conv_w"], p["conv_b"], layer=0, seq=seq, sh_idx=0,
            sc_idx=1, glu=False, out_dtype=F32)

    z, deltas = _hyena_features(seq, d)
    filt = _filters(z, p["f_w1"], p["f_b1"], p["f_w2"], p["f_b2"], p["f_freq"], p["f_w3"], deltas)

    m1, gb = _dft_tables(seq)
    m1b = m1.astype(BF16)
    minvb = (m1.T * (2.0 / n)).astype(BF16)
    gbb = gb.astype(BF16)
    gtb = jnp.swapaxes(gb, 1, 2).astype(BF16)
    kb = max(1, 256 // n1)
    td = _tile(d, MID_LANES)
    ts = _tile(d, DFT_LANES)
    nts = d // ts

    ncf = filt.shape[1]
    af = _dft1(filt.reshape(1, h2, n1, ncf), m1b, td=ts, n_blocks=ncf // ts, col_map=lambda c: c)
    kspec = _filter_spectrum(af[0], gbb, d=d, kb=kb, td=td)

    u4 = u.reshape(batch, h2, n1, 3 * d)
    zz4 = u4
    for o in range(HYENA_ORDER):
        a = _dft1(zz4, m1b, td=ts, n_blocks=nts, col_map=lambda c: c)
        bp = _conv_mid(a, kspec, o, gbb, gtb, kb=kb, td=td)
        zz4 = _idft1_gate(bp, minvb, u4, zz4, p["skip"][o:o + 1], td=ts, gate_part=1 + o, zz_part=0)
    return zz4.reshape(t_rows, d)


def _trunk(x3, modall, row0, p):
    batch, seq, d = x3.shape
    t_rows = batch * seq
    x = x3.reshape(t_rows, d)
    n_kv = d // HEAD_DIM // GQA_GROUP
    depth = modall.shape[0]
    for i in range(depth):
        mod = modall[i, row0:row0 + batch].reshape(batch, 6, d)
        kind, j = i % 3, i // 3
        g_pre, g_post = p["norm_mix_pre"][i:i + 1], p["norm_mix_post"][i:i + 1]
        if kind == 0:
            x = _pool_layer(x, mod, g_pre, g_post, p["pool_w"][j], p["pool_scale"][j:j + 1], seq=seq)
        elif kind == 1:
            hp = {k: v[j] for k, v in p["hy"].items()}
            zz = _hyena_mixer(x, mod, g_pre, hp, batch=batch, seq=seq)
            x = _down(zz, p["hy"]["w_out"], x, mod, g_post, layer=j, seq=seq, gate_idx=2)
        else:
            cos, sin = _rope_tables(seq)
            gains = jnp.stack([p["at_q_gain"][j], p["at_k_gain"][j]], axis=0)
            q, k, vt = _qkv(x, mod, g_pre, p["at_w_qkv"][j], gains, cos, sin, seq=seq, n_kv=n_kv)
            o = _flash(q, k, vt, batch=batch, seq=seq, n_kv=n_kv)
            x = _down(o, p["at_w_o"], x, mod, g_post, layer=j, seq=seq, gate_idx=2)
        g = _up(x, mod, p["norm_ffn_pre"][i:i + 1], p["ffn_w_up"], None, p["ffn_conv_w"][i],
                p["ffn_conv_b"][i:i + 1], layer=i, seq=seq, sh_idx=3, sc_idx=4, glu=True, out_dtype=BF16)
        x = _down(g, p["ffn_w_down"], x, mod, p["norm_ffn_post"][i:i + 1], layer=i, seq=seq, gate_idx=5)
    return x.reshape(batch, seq, d)


def kernel(x_prompt, x_sample, c_prompt, c_sample, mod_w, mod_b, norm_mix_pre, norm_mix_post, norm_ffn_pre,
           norm_ffn_post, ffn_w_up, ffn_conv_w, ffn_conv_b, ffn_w_down, pool_w, pool_scale, hy_w_in, hy_b_in,
           hy_conv_w, hy_conv_b, hy_f_w1, hy_f_b1, hy_f_w2, hy_f_b2, hy_f_w3, hy_f_freq, hy_skip, hy_w_out,
           at_w_qkv, at_q_gain, at_k_gain, at_w_o):
    nb = c_prompt.shape[0]
    ns = c_sample.shape[0]
    assert nb + ns <= MOD_ROWS
    c8 = jnp.concatenate([c_prompt, c_sample, jnp.zeros((MOD_ROWS - nb - ns, c_prompt.shape[1]), F32)], axis=0)
    modall = _modulation(c8, mod_w, mod_b)
    row = lambda v: v[:, None, :]
    p = dict(
        norm_mix_pre=norm_mix_pre, norm_mix_post=norm_mix_post, norm_ffn_pre=norm_ffn_pre,
        norm_ffn_post=norm_ffn_post,
        ffn_w_up=ffn_w_up.astype(BF16), ffn_conv_w=ffn_conv_w, ffn_conv_b=ffn_conv_b,
        ffn_w_down=ffn_w_down.astype(BF16),
        pool_w=pool_w.astype(BF16), pool_scale=pool_scale,
        hy=dict(
            w_in=hy_w_in.astype(BF16), b_in=row(hy_b_in), conv_w=hy_conv_w, conv_b=row(hy_conv_b),
            f_w1=jnp.pad(hy_f_w1, ((0, 0), (0, HYENA_EMB_PAD - HYENA_EMB), (0, 0))), f_b1=row(hy_f_b1),
            f_w2=hy_f_w2, f_b2=row(hy_f_b2), f_w3=hy_f_w3, f_freq=row(hy_f_freq), skip=hy_skip,
            w_out=hy_w_out.astype(BF16)),
        at_w_qkv=at_w_qkv.astype(BF16), at_q_gain=at_q_gain, at_k_gain=at_k_gain, at_w_o=at_w_o.astype(BF16),
    )
    y_prompt = _trunk(x_prompt, modall, 0, p)
    y_sample = _trunk(x_sample, modall, nb, p)
    return (y_prompt, y_sample)
```

```python
import functools
import math

import jax
import jax.numpy as jnp
from jax import lax
from jax.experimental import pallas as pl
from jax.experimental.pallas import tpu as pltpu

EPS = 1e-6
HEAD_DIM = 128
GQA_GROUP = 4
GRID_W = 64
ROPE_THETA = 10000.0
POOL_WINDOWS = (2, 4, 8, 16)
HYENA_ORDER = 2
HYENA_EMB = 33
HYENA_BANDS = (HYENA_EMB - 1) // 2
HYENA_EMB_PAD = 64
HYENA_FAST_DECAY = 0.3
HYENA_SLOW_DECAY = 1.5
HYENA_TARGET = 1e-2
HYENA_MOD_SHIFT = 0.05
HYENA_MIN_DECAY = math.log(HYENA_TARGET) / HYENA_SLOW_DECAY
HYENA_MAX_DECAY = math.log(HYENA_TARGET) / HYENA_FAST_DECAY

HALO = 16
MOD_ROWS = 8
DFT_N2_MAX = 512
DFT_N1_MIN = 16
DFT_ROWS = 16
IDFT_ROWS = 8
LANES = 128
UP_SUB = 512
FLASH_TK = 512
ONES_ROWS = 16
LOG2E = 1.4426950408889634
VMEM_LIMIT = 58 * 1024 * 1024

UP_TM = 1024
DOWN_TM = 512
DOWN_TK_MAX = 2816
POOL_TM = 512
QKV_TM = 512
FLASH_TQ = 512
MOD_TN = 1024
FILT_ROWS = 256
MID_LANES = 2048
DFT_LANES = 256

F32 = jnp.float32
BF16 = jnp.bfloat16


def _cparams(*sem):
    return pltpu.CompilerParams(dimension_semantics=sem, vmem_limit_bytes=VMEM_LIMIT)


def _tile(dim, pref):
    t = min(dim, pref)
    while dim % t:
        t //= 2
    return t


def _lane_tile(dim, cap):
    units = dim // LANES
    best = max(u for u in range(1, units + 1) if units % u == 0 and u * LANES <= max(cap, LANES))
    return best * LANES


def _split3(x):
    hi = x.astype(BF16)
    lo = (x - hi.astype(F32)).astype(BF16)
    return hi, lo


def _dot3(a, b):
    ah, al = _split3(a)
    bh, bl = _split3(b)
    d = functools.partial(jnp.dot, preferred_element_type=F32)
    return d(ah, bh) + (d(ah, bl) + d(al, bh))


def _dft_dot(m, x):
    return jnp.dot(m, x.astype(BF16), preferred_element_type=F32)


def _modnorm(xv, gain, scale1p, shift):
    ms = jnp.mean(xv * xv, axis=-1, keepdims=True)
    return xv * lax.rsqrt(ms + EPS) * gain * scale1p + shift


def _halo_maps(tm, t_rows):
    r = tm // HALO
    last = t_rows // HALO - 1
    prev = lambda i, *_: (jnp.maximum(i * r - 1, 0), 0)
    nxt = lambda i, *_: (jnp.minimum((i + 1) * r, last), 0)
    return prev, nxt


def _mod_kernel(c_ref, w_ref, b_ref, o_ref):
    c = c_ref[...]
    a = (c / (1.0 + jnp.exp(-c))).astype(BF16)
    o_ref[0] = jnp.dot(a, w_ref[0].astype(BF16), preferred_element_type=F32) + b_ref[0]


def _modulation(c8, mod_w, mod_b):
    depth, d, n = mod_w.shape
    tn = _tile(n, MOD_TN)
    return pl.pallas_call(
        _mod_kernel,
        out_shape=jax.ShapeDtypeStruct((depth, MOD_ROWS, n), F32),
        grid=(depth, n // tn),
        in_specs=[
            pl.BlockSpec((MOD_ROWS, d), lambda l, j: (0, 0)),
            pl.BlockSpec((1, d, tn), lambda l, j: (l, 0, j)),
            pl.BlockSpec((1, 1, tn), lambda l, j: (l, 0, j)),
        ],
        out_specs=pl.BlockSpec((1, MOD_ROWS, tn), lambda l, j: (l, 0, j)),
        compiler_params=_cparams("arbitrary", "arbitrary"),
        name="modulation",
    )(c8, mod_w, mod_b.reshape(depth, 1, n))


def _fill_h(h_scr, x_ref, xp_ref, xn_ref, mod_ref, g_ref, sh_idx, sc_idx, tm):
    m = mod_ref[0]
    shift = m[sh_idx:sh_idx + 1]
    scale1p = 1.0 + m[sc_idx:sc_idx + 1]
    gain = g_ref[...]
    h_scr[0:HALO] = _modnorm(xp_ref[...], gain, scale1p, shift).astype(h_scr.dtype)
    h_scr[HALO:HALO + tm] = _modnorm(x_ref[...], gain, scale1p, shift).astype(h_scr.dtype)
    h_scr[HALO + tm:] = _modnorm(xn_ref[...], gain, scale1p, shift).astype(h_scr.dtype)


def _edge_keep(tps):
    i = pl.program_id(0) % tps
    return (i != 0).astype(F32), (i != tps - 1).astype(F32)


def _conv3_rows(a, keep_first, keep_last, cw, cb, tm):
    rows = tm + 2 * HALO
    a = jnp.concatenate([a[0:HALO] * keep_first, a[HALO:HALO + tm], a[HALO + tm:] * keep_last], axis=0)
    prev = pltpu.roll(a, 1, 0)
    nxt = pltpu.roll(a, rows - 1, 0)
    c = prev * cw[0:1] + a * cw[1:2] + nxt * cw[2:3] + cb
    return c[HALO:HALO + tm]


def _gelu_tanh(x):
    return 0.5 * x * (1.0 + jnp.tanh(0.7978845608028654 * (x + 0.044715 * (x * x * x))))


def _up_kernel(*refs, sh_idx, sc_idx, tm, tps, glu, has_bias, n_chunks):
    x_ref, xp_ref, xn_ref, mod_ref, g_ref, w_ref = refs[:6]
    k = 6
    wb_ref = b_ref = None
    if glu:
        wb_ref = refs[k]
        k += 1
    if has_bias:
        b_ref = refs[k]
        k += 1
    cw_ref, cb_ref, o_ref, h_scr, hn_scr = refs[k:k + 5]
    i = pl.program_id(0)
    j = pl.program_id(1)

    @pl.when(jnp.logical_and(j == 0, i == 0))
    def _():
        _fill_h(h_scr, x_ref, xp_ref, xn_ref, mod_ref, g_ref, sh_idx, sc_idx, tm)

    @pl.when(jnp.logical_and(j == 0, i > 0))
    def _():
        h_scr[...] = hn_scr[...]

    m = mod_ref[0]
    shift = m[sh_idx:sh_idx + 1]
    scale1p = 1.0 + m[sc_idx:sc_idx + 1]
    gain = g_ref[...]

    @pl.when(j == 1)
    def _():
        hn_scr[0:HALO] = _modnorm(xp_ref[...], gain, scale1p, shift).astype(hn_scr.dtype)
        hn_scr[HALO + tm:] = _modnorm(xn_ref[...], gain, scale1p, shift).astype(hn_scr.dtype)

    rc = tm // n_chunks
    chunk = jnp.clip(j - 1, 0, n_chunks - 1)
    r0 = pl.multiple_of(chunk * rc, HALO)
    hn_scr[pl.ds(pl.multiple_of(HALO + r0, HALO), rc), :] = _modnorm(
        x_ref[pl.ds(r0, rc), :], gain, scale1p, shift).astype(hn_scr.dtype)

    keep_first, keep_last = _edge_keep(tps)
    for s in range(o_ref.shape[1] // UP_SUB):
        sl = slice(s * UP_SUB, (s + 1) * UP_SUB)
        a = jnp.dot(h_scr[...], w_ref[:, sl], preferred_element_type=F32)
        if has_bias:
            a = a + b_ref[:, sl]
        c = _conv3_rows(a, keep_first, keep_last, cw_ref[:, sl], cb_ref[:, sl], tm)
        if glu:
            b = jnp.dot(h_scr[HALO:HALO + tm], wb_ref[:, sl], preferred_element_type=F32)
            c = _gelu_tanh(c) * b
        o_ref[:, sl] = c.astype(o_ref.dtype)


def _up(x, mod, gain, w, bias, cw, cb, *, layer, seq, sh_idx, sc_idx, glu, out_dtype):
    t_rows, d = x.shape
    n_out = cw.shape[-1]
    tm = _tile(seq, UP_TM)
    tn = _tile(n_out, UP_SUB if glu else 2 * UP_SUB)
    assert tn % UP_SUB == 0
    tps = seq // tm
    nj = n_out // tn
    n_tiles = t_rows // tm
    assert nj >= 2
    n_chunks = 1
    while 2 * n_chunks <= nj - 1 and tm % (2 * n_chunks * HALO) == 0:
        n_chunks *= 2
    r = tm // HALO

    def ahead(i, j):
        return jnp.minimum(i + jnp.where(jnp.logical_and(i == 0, j == 0), 0, 1), n_tiles - 1)

    in_specs = [
        pl.BlockSpec((tm, d), lambda i, j: (ahead(i, j), 0)),
        pl.BlockSpec((HALO, d), lambda i, j: (jnp.maximum(ahead(i, j) * r - 1, 0), 0)),
        pl.BlockSpec((HALO, d), lambda i, j: (jnp.minimum((ahead(i, j) + 1) * r, t_rows // HALO - 1), 0)),
        pl.BlockSpec((1, 6, d), lambda i, j: (ahead(i, j) // tps, 0, 0)),
        pl.BlockSpec((1, d), lambda i, j: (0, 0)),
        pl.BlockSpec((None, d, tn), lambda i, j: (layer, 0, j)),
    ]
    args = [x, x, x, mod, gain, w]
    if glu:
        in_specs.append(pl.BlockSpec((None, d, tn), lambda i, j: (layer, 0, j + nj)))
        args.append(w)
    if bias is not None:
        in_specs.append(pl.BlockSpec((1, tn), lambda i, j: (0, j)))
        args.append(bias)
    in_specs += [pl.BlockSpec((3, tn), lambda i, j: (0, j)), pl.BlockSpec((1, tn), lambda i, j: (0, j))]
    args += [cw, cb]
    return pl.pallas_call(
        functools.partial(_up_kernel, sh_idx=sh_idx, sc_idx=sc_idx, tm=tm, tps=tps, glu=glu,
                          has_bias=bias is not None, n_chunks=n_chunks),
        out_shape=jax.ShapeDtypeStruct((t_rows, n_out), out_dtype),
        grid=(n_tiles, nj),
        in_specs=in_specs,
        out_specs=pl.BlockSpec((tm, tn), lambda i, j: (i, j)),
        scratch_shapes=[pltpu.VMEM((tm + 2 * HALO, d), BF16), pltpu.VMEM((tm + 2 * HALO, d), BF16)],
        compiler_params=_cparams("arbitrary", "arbitrary"),
        name="up_glu" if glu else "up_conv",
    )(*args)


def _down_kernel(g_ref, w_ref, x_ref, mod_ref, gain_ref, o_ref, *scratch, gate_idx, nk):
    k = pl.program_id(1)
    acc_ref = scratch[0] if nk > 1 else None
    part = jnp.dot(g_ref[...].astype(w_ref.dtype), w_ref[...], preferred_element_type=F32)

    if nk > 1:
        @pl.when(k == 0)
        def _():
            acc_ref[...] = part

    if nk > 2:
        @pl.when(jnp.logical_and(k > 0, k < nk - 1))
        def _():
            acc_ref[...] += part

    @pl.when(k == nk - 1)
    def _():
        f = acc_ref[...] + part if nk > 1 else part
        ms = jnp.mean(f * f, axis=-1, keepdims=True)
        y = f * lax.rsqrt(ms + EPS) * gain_ref[...]
        gate = mod_ref[0][gate_idx:gate_idx + 1]
        o_ref[...] = x_ref[...] + gate * y


def _down(g, w, x, mod, gain, *, layer, seq, gate_idx):
    t_rows, kdim = g.shape
    d = w.shape[2]
    tm = _tile(seq, DOWN_TM)
    tk = _lane_tile(kdim, DOWN_TK_MAX)
    tps = seq // tm
    nk = kdim // tk
    return pl.pallas_call(
        functools.partial(_down_kernel, gate_idx=gate_idx, nk=nk),
        out_shape=jax.ShapeDtypeStruct((t_rows, d), F32),
        grid=(t_rows // tm, nk),
        in_specs=[
            pl.BlockSpec((tm, tk), lambda i, k: (i, k)),
            pl.BlockSpec((None, tk, d), lambda i, k: (layer, k, 0)),
            pl.BlockSpec((tm, d), lambda i, k: (i, 0)),
            pl.BlockSpec((1, 6, d), lambda i, k: (i // tps, 0, 0)),
            pl.BlockSpec((1, d), lambda i, k: (0, 0)),
        ],
        out_specs=pl.BlockSpec((tm, d), lambda i, k: (i, 0)),
        scratch_shapes=[pltpu.VMEM((tm, d), F32)] if nk > 1 else [],
        compiler_params=_cparams("arbitrary", "arbitrary"),
        name="down",
    )(g, w, x, mod, gain)


def _pool_kernel(x_ref, xp_ref, xn_ref, mod_ref, gpre_ref, gpost_ref, pw_ref, ps_ref, o_ref, h_scr,
                 *, tm, tps, seq):
    _fill_h(h_scr, x_ref, xp_ref, xn_ref, mod_ref, gpre_ref, 0, 1, tm)
    keep_first, keep_last = _edge_keep(tps)
    rows = tm + 2 * HALO
    d = x_ref.shape[1]
    cg = d // len(POOL_WINDOWS)
    pos = (pl.program_id(0) % tps) * tm + lax.broadcasted_iota(jnp.int32, (tm, 1), 0)
    ys = []
    ssq = jnp.zeros((tm, 1), F32)
    for g, win in enumerate(POOL_WINDOWS):
        half = win // 2
        sl = slice(g * cg, (g + 1) * cg)
        hg = jnp.concatenate([h_scr[0:HALO, sl] * keep_first, h_scr[HALO:HALO + tm, sl],
                              h_scr[HALO + tm:, sl] * keep_last], axis=0)
        p = hg
        s = 1
        while s < win:
            p = p + pltpu.roll(p, s, 0)
            s *= 2
        if half > 1:
            p = pltpu.roll(p, rows - (half - 1), 0)
        lo = jnp.maximum(pos - half, 0)
        hi = jnp.minimum(pos + (half - 1), seq - 1)
        inv_cnt = 1.0 / (hi - lo + 1).astype(F32)
        pooled = p[HALO:HALO + tm] * inv_cnt - hg[HALO:HALO + tm]
        y = jnp.dot(pooled.astype(BF16), pw_ref[g], preferred_element_type=F32) * ps_ref[:, sl]
        ssq = ssq + jnp.sum(y * y, axis=-1, keepdims=True)
        ys.append(y)
    inv = lax.rsqrt(ssq / d + EPS)
    gate = mod_ref[0][2:3]
    for g in range(len(POOL_WINDOWS)):
        sl = slice(g * cg, (g + 1) * cg)
        o_ref[:, sl] = x_ref[:, sl] + gate[:, sl] * (ys[g] * inv * gpost_ref[:, sl])


def _pool_layer(x, mod, gpre, gpost, pw, ps, *, seq):
    t_rows, d = x.shape
    tm = _tile(seq, POOL_TM)
    tps = seq // tm
    prev, nxt = _halo_maps(tm, t_rows)
    return pl.pallas_call(
        functools.partial(_pool_kernel, tm=tm, tps=tps, seq=seq),
        out_shape=jax.ShapeDtypeStruct((t_rows, d), F32),
        grid=(t_rows // tm,),
        in_specs=[
            pl.BlockSpec((tm, d), lambda i: (i, 0)),
            pl.BlockSpec((HALO, d), prev),
            pl.BlockSpec((HALO, d), nxt),
            pl.BlockSpec((1, 6, d), lambda i: (i // tps, 0, 0)),
            pl.BlockSpec((1, d), lambda i: (0, 0)),
            pl.BlockSpec((1, d), lambda i: (0, 0)),
            pl.BlockSpec(pw.shape, lambda i: (0, 0, 0)),
            pl.BlockSpec((1, d), lambda i: (0, 0)),
        ],
        out_specs=pl.BlockSpec((tm, d), lambda i: (i, 0)),
        scratch_shapes=[pltpu.VMEM((tm + 2 * HALO, d), F32)],
        compiler_params=_cparams("arbitrary"),
        name="pool_layer",
    )(x, x, x, mod, gpre, gpost, pw, ps)


def _rope_tables(seq):
    pos = jnp.arange(seq, dtype=jnp.int32)
    row = (pos // GRID_W).astype(F32)[:, None]
    col = (pos % GRID_W).astype(F32)[:, None]
    axis_dim = HEAD_DIM // 2
    inv = ROPE_THETA ** (-jnp.arange(0, axis_dim, 2, dtype=F32) / axis_dim)
    ar, ac = row * inv, col * inv
    cos = jnp.concatenate([jnp.cos(ar), jnp.cos(ar), jnp.cos(ac), jnp.cos(ac)], axis=-1)
    sin = jnp.concatenate([-jnp.sin(ar), jnp.sin(ar), -jnp.sin(ac), jnp.sin(ac)], axis=-1)
    return cos, sin


def _qkv_kernel(x_ref, mod_ref, g_ref, w0_ref, w1_ref, hg_ref, cos_ref, sin_ref, q_ref, k_ref, vt_ref, h_scr, *,
                n_q_steps):
    j = pl.program_id(1)
    tn = k_ref.shape[1]

    @pl.when(j == 0)
    def _():
        m = mod_ref[0]
        h_scr[...] = _modnorm(x_ref[...], g_ref[...], 1.0 + m[1:2], m[0:1]).astype(BF16)

    def proj(t):
        return jnp.dot(h_scr[...], (w0_ref, w1_ref)[t][...], preferred_element_type=F32)

    def norm_rope(a, gain, out_scale, o_ref, col0):
        cos = cos_ref[...]
        sin = sin_ref[...]
        quarter = HEAD_DIM // 4
        lane = lax.broadcasted_iota(jnp.int32, (1, HEAD_DIM), 1)
        low = (lane % (2 * quarter)) < quarter
        for h in range(tn // HEAD_DIM):
            v = a[:, h * HEAD_DIM:(h + 1) * HEAD_DIM]
            ms = jnp.mean(v * v, axis=-1, keepdims=True)
            v = v * lax.rsqrt(ms + EPS) * gain
            swapped = jnp.where(low, pltpu.roll(v, HEAD_DIM - quarter, 1), pltpu.roll(v, quarter, 1))
            o_ref[:, col0 + h * HEAD_DIM:col0 + (h + 1) * HEAD_DIM] = (
                (v * cos + swapped * sin) * out_scale).astype(o_ref.dtype)

    @pl.when(j < n_q_steps)
    def _():
        for t in range(2):
            norm_rope(proj(t), hg_ref[0:1], HEAD_DIM ** -0.5 * LOG2E, q_ref, t * tn)

    @pl.when(j == n_q_steps)
    def _():
        norm_rope(proj(0), hg_ref[1:2], 1.0, k_ref, 0)
        v = proj(1)
        for c in range(vt_ref.shape[0]):
            vt_ref[c] = v[c * FLASH_TK:(c + 1) * FLASH_TK].T.astype(vt_ref.dtype)


def _qkv(x, mod, gain, w, head_gains, cos, sin, *, seq, n_kv):
    t_rows, d = x.shape
    n_out = w.shape[1]
    tm = _tile(seq, QKV_TM)
    tn = HEAD_DIM * n_kv
    tps = seq // tm
    n_q_steps = d // (2 * tn)
    assert tm % FLASH_TK == 0 and n_out == d + 2 * tn and d % (2 * tn) == 0
    return pl.pallas_call(
        functools.partial(_qkv_kernel, n_q_steps=n_q_steps),
        out_shape=(jax.ShapeDtypeStruct((t_rows, d), BF16), jax.ShapeDtypeStruct((t_rows, tn), BF16),
                   jax.ShapeDtypeStruct((t_rows // FLASH_TK, tn, FLASH_TK), BF16)),
        grid=(t_rows // tm, n_q_steps + 1),
        in_specs=[
            pl.BlockSpec((tm, d), lambda i, j: (i, 0)),
            pl.BlockSpec((1, 6, d), lambda i, j: (i // tps, 0, 0)),
            pl.BlockSpec((1, d), lambda i, j: (0, 0)),
            pl.BlockSpec((d, tn), lambda i, j: (0, 2 * j)),
            pl.BlockSpec((d, tn), lambda i, j: (0, 2 * j + 1)),
            pl.BlockSpec((2, HEAD_DIM), lambda i, j: (0, 0)),
            pl.BlockSpec((tm, HEAD_DIM), lambda i, j: (i % tps, 0)),
            pl.BlockSpec((tm, HEAD_DIM), lambda i, j: (i % tps, 0)),
        ],
        out_specs=(
            pl.BlockSpec((tm, 2 * tn), lambda i, j: (i, jnp.minimum(j, n_q_steps - 1))),
            pl.BlockSpec((tm, tn), lambda i, j: (i, 0)),
            pl.BlockSpec((tm // FLASH_TK, tn, FLASH_TK), lambda i, j: (i, 0, 0)),
        ),
        scratch_shapes=[pltpu.VMEM((tm, d), BF16)],
        compiler_params=_cparams("arbitrary", "arbitrary"),
        name="qkv_rope",
    )(x, mod, gain, w, w, head_gains, cos, sin)


def _flash_kernel(q_ref, k_ref, vt_ref, o_ref, acc_ref, s_ref, *, tq, nk):
    q = q_ref[...]
    qs = jnp.concatenate([q[:, h * HEAD_DIM:(h + 1) * HEAD_DIM] for h in range(GQA_GROUP)], axis=0)
    ones = jnp.ones((ONES_ROWS, FLASH_TK), BF16)
    acc_ref[...] = jnp.zeros_like(acc_ref)

    def scores(kk, slot):
        start = pl.multiple_of(kk * FLASH_TK, FLASH_TK)
        s_ref[slot] = lax.dot_general(k_ref[pl.ds(start, FLASH_TK), :], qs, (((1,), (1,)), ((), ())),
                                      preferred_element_type=F32)

    def accumulate(kk, slot, m):
        va = jnp.concatenate([vt_ref[kk], ones], axis=0)
        m_out = []
        for c in range(GQA_GROUP * tq // UP_SUB):
            sl = slice(c * UP_SUB, (c + 1) * UP_SUB)
            st = s_ref[slot, :, sl]
            m_new = jnp.maximum(m[:, sl], jnp.max(st, axis=0, keepdims=True))
            alpha = jnp.exp2(m[:, sl] - m_new)
            pt = jnp.exp2(st - m_new).astype(BF16)
            acc_ref[:, sl] = alpha * acc_ref[:, sl] + jnp.dot(va, pt, preferred_element_type=F32)
            m_out.append(m_new)
        return jnp.concatenate(m_out, axis=1)

    scores(0, 0)
    unroll = 4 if nk % 4 == 0 else 2

    def group(i, m):
        k0 = unroll * i
        for u in range(unroll):
            nxt = k0 + u + 1
            scores(nxt if u + 1 < unroll else jnp.minimum(nxt, nk - 1), (u + 1) % 2)
            m = accumulate(k0 + u, u % 2, m)
        return m

    lax.fori_loop(0, nk // unroll, group, jnp.full((1, GQA_GROUP * tq), -jnp.inf, F32))
    acc = acc_ref[...]
    ot = acc[:HEAD_DIM] / acc[HEAD_DIM:HEAD_DIM + 1]
    for h in range(GQA_GROUP):
        o_ref[:, h * HEAD_DIM:(h + 1) * HEAD_DIM] = ot[:, h * tq:(h + 1) * tq].T.astype(o_ref.dtype)


def _flash(q, k, vt, *, batch, seq, n_kv):
    t_rows, d = q.shape
    tq = _tile(seq, FLASH_TQ)
    nqb = seq // tq
    nk = seq // FLASH_TK
    assert nk % 2 == 0
    gw = GQA_GROUP * HEAD_DIM
    return pl.pallas_call(
        functools.partial(_flash_kernel, tq=tq, nk=nk),
        out_shape=jax.ShapeDtypeStruct((t_rows, d), BF16),
        grid=(batch, n_kv, nqb),
        in_specs=[
            pl.BlockSpec((tq, gw), lambda b, kv, qi: (b * nqb + qi, kv)),
            pl.BlockSpec((seq, HEAD_DIM), lambda b, kv, qi: (b, kv)),
            pl.BlockSpec((nk, HEAD_DIM, FLASH_TK), lambda b, kv, qi: (b, kv, 0)),
        ],
        out_specs=pl.BlockSpec((tq, gw), lambda b, kv, qi: (b * nqb + qi, kv)),
        scratch_shapes=[pltpu.VMEM((HEAD_DIM + ONES_ROWS, GQA_GROUP * tq), F32),
                        pltpu.VMEM((2, FLASH_TK, GQA_GROUP * tq), F32)],
        compiler_params=_cparams("arbitrary", "arbitrary", "arbitrary"),
        name="flash_attention",
    )(q, k, vt)


def _dft_sizes(seq):
    n = 2 * seq
    n2 = DFT_N2_MAX
    while n // n2 < DFT_N1_MIN:
        n2 //= 2
    return n, n // n2, n2


def _dft_tables(seq):
    n, n1, n2 = _dft_sizes(seq)
    h2 = n2 // 2
    k2 = jnp.arange(h2, dtype=jnp.int32)
    m2 = jnp.arange(h2, dtype=jnp.int32)
    idx = (m2[None, :] * (2 * k2[:, None] + 1)) % (2 * n2)
    th = idx.astype(F32) * (2.0 * math.pi / (2 * n2))
    m1 = jnp.concatenate([jnp.cos(th), -jnp.sin(th)], axis=0)
    a1 = jnp.arange(n1, dtype=jnp.int32)
    k1 = jnp.arange(n1, dtype=jnp.int32)
    freq = 2 * n2 * k1[None, :, None] + 2 * k2[:, None, None] + 1
    idx = (a1[None, None, :] * freq) % (2 * n)
    ph = idx.astype(F32) * (2.0 * math.pi / (2 * n))
    c, s = jnp.cos(ph), jnp.sin(ph)
    gb = jnp.concatenate([jnp.concatenate([c, s], axis=2), jnp.concatenate([-s, c], axis=2)], axis=1)
    return m1, gb


def _filt_kernel(z_ref, w1_ref, b1_ref, w2_ref, b2_ref, fr_ref, w3_ref, dl_ref, o_ref):
    z = z_ref[...]
    fr = fr_ref[...]
    a = jnp.sin(fr * (_dot3(z, w1_ref[...]) + b1_ref[...]))
    a = jnp.sin(fr * (_dot3(a, w2_ref[...]) + b2_ref[...]))
    decay = jnp.exp(-z[:, 0:1] * dl_ref[...]) + HYENA_MOD_SHIFT
    d = decay.shape[1]
    row = pl.program_id(0) * z.shape[0] + lax.broadcasted_iota(jnp.int32, (z.shape[0], 1), 0)
    for c in range(w3_ref.shape[1] // d):
        f = _dot3(a, w3_ref[:, c * d:(c + 1) * d]) * decay
        o_ref[:, c * d:(c + 1) * d] = jnp.where(row == 0, 0.0, f) if c % 2 == 1 else f


def _filters(z, w1, b1, w2, b2, fr, w3, deltas):
    seq = z.shape[0]
    d = deltas.shape[1]
    tl = _tile(seq, FILT_ROWS)
    hid = w2.shape[0]
    const = lambda i: (0, 0)
    return pl.pallas_call(
        _filt_kernel,
        out_shape=jax.ShapeDtypeStruct((seq, w3.shape[1]), F32),
        grid=(seq // tl,),
        in_specs=[
            pl.BlockSpec((tl, HYENA_EMB_PAD), lambda i: (i, 0)),
            pl.BlockSpec((HYENA_EMB_PAD, hid), const),
            pl.BlockSpec((1, hid), const),
            pl.BlockSpec((hid, hid), const),
            pl.BlockSpec((1, hid), const),
            pl.BlockSpec((1, hid), const),
            pl.BlockSpec(w3.shape, const),
            pl.BlockSpec((1, d), const),
        ],
        out_specs=pl.BlockSpec((tl, w3.shape[1]), lambda i: (i, 0)),
        compiler_params=_cparams("arbitrary"),
        name="hyena_filters",
    )(z, w1, b1, w2, b2, fr, w3, deltas)


def _dft1_kernel(x_ref, m_ref, o_ref, y_scr):
    xt = pltpu.einshape("hjd->jhd", x_ref[0])
    m = m_ref[...]
    for j in range(DFT_ROWS):
        y_scr[j] = _dft_dot(m, xt[j])
    yt = pltpu.einshape("jnd->njd", y_scr[...]).astype(o_ref.dtype)
    h2 = yt.shape[0] // 2
    o_ref[0, 0] = yt[:h2]
    o_ref[0, 1] = yt[h2:]


def _dft1(x4, m1b, *, td, n_blocks, col_map):
    b, h2, n1, _ = x4.shape
    n2 = m1b.shape[0]
    return pl.pallas_call(
        _dft1_kernel,
        out_shape=jax.ShapeDtypeStruct((b, 2, h2, n1, n_blocks * td), BF16),
        grid=(b, n1 // DFT_ROWS, n_blocks),
        in_specs=[
            pl.BlockSpec((1, h2, DFT_ROWS, td), lambda bi, g, c: (bi, 0, g, col_map(c))),
            pl.BlockSpec(m1b.shape, lambda bi, g, c: (0, 0)),
        ],
        out_specs=pl.BlockSpec((1, 2, h2, DFT_ROWS, td), lambda bi, g, c: (bi, 0, 0, g, c)),
        scratch_shapes=[pltpu.VMEM((DFT_ROWS, n2, td), F32)],
        compiler_params=_cparams("arbitrary", "arbitrary", "arbitrary"),
        name="dft_stage1",
    )(x4, m1b)


def _fspec_kernel(af_ref, ab_ref, g_ref, o_ref, *, kb, n1):
    xf = [_dft_dot(g_ref[kk], jnp.concatenate([af_ref[0, kk], af_ref[1, kk]], axis=0)) for kk in range(kb)]
    xb = [_dft_dot(g_ref[kk], jnp.concatenate([ab_ref[0, kk], ab_ref[1, kk]], axis=0)) for kk in range(kb)]
    for kk in range(kb):
        o_ref[0, 0, kk] = (xf[kk][:n1] + xb[kk][:n1]).astype(o_ref.dtype)
        o_ref[0, 1, kk] = (xf[kk][n1:] - xb[kk][n1:]).astype(o_ref.dtype)


def _filter_spectrum(af, gbb, *, d, kb, td):
    _, h2, n1, w = af.shape
    ndt = d // td
    norder = w // (2 * d)
    return pl.pallas_call(
        functools.partial(_fspec_kernel, kb=kb, n1=n1),
        out_shape=jax.ShapeDtypeStruct((norder, 2, h2, n1, d), BF16),
        grid=(norder, h2 // kb, ndt),
        in_specs=[
            pl.BlockSpec((2, kb, n1, td), lambda o, k, t: (0, k, 0, (2 * o) * ndt + t)),
            pl.BlockSpec((2, kb, n1, td), lambda o, k, t: (0, k, 0, (2 * o + 1) * ndt + t)),
            pl.BlockSpec((kb, 2 * n1, 2 * n1), lambda o, k, t: (k, 0, 0)),
        ],
        out_specs=pl.BlockSpec((1, 2, kb, n1, td), lambda o, k, t: (o, 0, k, 0, t)),
        compiler_params=_cparams("arbitrary", "arbitrary", "arbitrary"),
        name="hyena_filter_spectrum",
    )(af, af, gbb)


def _cmid_kernel(a_ref, ks_ref, g_ref, gt_ref, o_ref, *, kb, n1):
    xs = [_dft_dot(g_ref[kk], jnp.concatenate([a_ref[0, 0, kk], a_ref[0, 1, kk]], axis=0)) for kk in range(kb)]
    ys = []
    for kk in range(kb):
        xr, xi = xs[kk][:n1], xs[kk][n1:]
        kr, ki = ks_ref[0, 0, kk], ks_ref[0, 1, kk]
        ys.append(jnp.concatenate([xr * kr - xi * ki, xr * ki + xi * kr], axis=0))
    bps = [_dft_dot(gt_ref[kk], ys[kk]) for kk in range(kb)]
    for kk in range(kb):
        o_ref[0, 0, kk] = bps[kk][:n1]
        o_ref[0, 1, kk] = bps[kk][n1:]


def _conv_mid(a, kspec, order, gbb, gtb, *, kb, td):
    b, _, h2, n1, d = a.shape
    return pl.pallas_call(
        functools.partial(_cmid_kernel, kb=kb, n1=n1),
        out_shape=jax.ShapeDtypeStruct(a.shape, F32),
        grid=(h2 // kb, d // td, b),
        in_specs=[
            pl.BlockSpec((1, 2, kb, n1, td), lambda k, t, bi: (bi, 0, k, 0, t)),
            pl.BlockSpec((1, 2, kb, n1, td), lambda k, t, bi: (order, 0, k, 0, t)),
            pl.BlockSpec((kb, 2 * n1, 2 * n1), lambda k, t, bi: (k, 0, 0)),
            pl.BlockSpec((kb, 2 * n1, 2 * n1), lambda k, t, bi: (k, 0, 0)),
        ],
        out_specs=pl.BlockSpec((1, 2, kb, n1, td), lambda k, t, bi: (bi, 0, k, 0, t)),
        compiler_params=_cparams("arbitrary", "arbitrary", "arbitrary"),
        name="hyena_conv_mid",
    )(a, kspec, gbb, gtb)


def _idft1_kernel(bp_ref, m_ref, gate_ref, zz_ref, skip_ref, o_ref, y_scr):
    bp = jnp.concatenate([bp_ref[0, 0], bp_ref[0, 1]], axis=0)
    bt = pltpu.einshape("njd->jnd", bp)
    m = m_ref[...]
    for j in range(IDFT_ROWS):
        y_scr[j] = _dft_dot(m, bt[j])
    y = pltpu.einshape("jhd->hjd", y_scr[...])
    o_ref[0] = (gate_ref[0] * (y + zz_ref[0] * skip_ref[...])).astype(o_ref.dtype)


def _idft1_gate(bp, minvb, u4, zz4, skip, *, td, gate_part, zz_part):
    b, _, h2, n1, d = bp.shape
    nt = d // td
    return pl.pallas_call(
        _idft1_kernel,
        out_shape=jax.ShapeDtypeStruct((b, h2, n1, d), F32),
        grid=(b, n1 // IDFT_ROWS, nt),
        in_specs=[
            pl.BlockSpec((1, 2, h2, IDFT_ROWS, td), lambda bi, g, c: (bi, 0, 0, g, c)),
            pl.BlockSpec(minvb.shape, lambda bi, g, c: (0, 0)),
            pl.BlockSpec((1, h2, IDFT_ROWS, td), lambda bi, g, c: (bi, 0, g, gate_part * nt + c)),
            pl.BlockSpec((1, h2, IDFT_ROWS, td), lambda bi, g, c: (bi, 0, g, zz_part * nt + c)),
            pl.BlockSpec((1, td), lambda bi, g, c: (0, c)),
        ],
        out_specs=pl.BlockSpec((1, h2, IDFT_ROWS, td), lambda bi, g, c: (bi, 0, g, c)),
        scratch_shapes=[pltpu.VMEM((IDFT_ROWS, h2, td), F32)],
        compiler_params=_cparams("arbitrary", "arbitrary", "arbitrary"),
        name="idft_stage1_gate",
    )(bp, minvb, u4, zz4, skip)


def _hyena_features(seq, d):
    t = jnp.linspace(0.0, 1.0, seq, dtype=F32)[:, None]
    w = 2.0 * math.pi * jnp.arange(seq, dtype=F32)[:, None] / seq
    f = jnp.linspace(1e-4, HYENA_BANDS - 1, HYENA_BANDS, dtype=F32)[None, :]
    z = jnp.concatenate([t, jnp.cos(f * w), -jnp.sin(f * w)], axis=-1)
    z = jnp.pad(z, ((0, 0), (0, HYENA_EMB_PAD - HYENA_EMB)))
    deltas = jnp.abs(jnp.linspace(HYENA_MIN_DECAY, HYENA_MAX_DECAY, d, dtype=F32))[None, :]
    return z, deltas


def _hyena_mixer(x, mod, gain, p, *, batch, seq):
    t_rows, d = x.shape
    n, n1, n2 = _dft_sizes(seq)
    h2 = n2 // 2
    u = _up(x, mod, gain, p["w_in"][None], p["b_in"], p["conv_w"], p["conv_b"], layer=0, seq=seq, sh_idx=0,
            sc_idx=1, glu=False, out_dtype=F32)

    z, deltas = _hyena_features(seq, d)
    filt = _filters(z, p["f_w1"], p["f_b1"], p["f_w2"], p["f_b2"], p["f_freq"], p["f_w3"], deltas)

    m1, gb = _dft_tables(seq)
    m1b = m1.astype(BF16)
    minvb = (m1.T * (2.0 / n)).astype(BF16)
    gbb = gb.astype(BF16)
    gtb = jnp.swapaxes(gb, 1, 2).astype(BF16)
    kb = max(1, 256 // n1)
    td = _tile(d, MID_LANES)
    ts = _tile(d, DFT_LANES)
    nts = d // ts

    ncf = filt.shape[1]
    af = _dft1(filt.reshape(1, h2, n1, ncf), m1b, td=ts, n_blocks=ncf // ts, col_map=lambda c: c)
    kspec = _filter_spectrum(af[0], gbb, d=d, kb=kb, td=td)

    u4 = u.reshape(batch, h2, n1, 3 * d)
    zz4 = u4
    for o in range(HYENA_ORDER):
        a = _dft1(zz4, m1b, td=ts, n_blocks=nts, col_map=lambda c: c)
        bp = _conv_mid(a, kspec, o, gbb, gtb, kb=kb, td=td)
        zz4 = _idft1_gate(bp, minvb, u4, zz4, p["skip"][o:o + 1], td=ts, gate_part=1 + o, zz_part=0)
    return zz4.reshape(t_rows, d)


def _trunk(x3, modall, row0, p):
    batch, seq, d = x3.shape
    t_rows = batch * seq
    x = x3.reshape(t_rows, d)
    n_kv = d // HEAD_DIM // GQA_GROUP
    depth = modall.shape[0]
    for i in range(depth):
        mod = modall[i, row0:row0 + batch].reshape(batch, 6, d)
        kind, j = i % 3, i // 3
        g_pre, g_post = p["norm_mix_pre"][i:i + 1], p["norm_mix_post"][i:i + 1]
        if kind == 0:
            x = _pool_layer(x, mod, g_pre, g_post, p["pool_w"][j], p["pool_scale"][j:j + 1], seq=seq)
        elif kind == 1:
            hp = {k: v[j] for k, v in p["hy"].items()}
            zz = _hyena_mixer(x, mod, g_pre, hp, batch=batch, seq=seq)
            x = _down(zz, p["hy"]["w_out"], x, mod, g_post, layer=j, seq=seq, gate_idx=2)
        else:
            cos, sin = _rope_tables(seq)
            gains = jnp.stack([p["at_q_gain"][j], p["at_k_gain"][j]], axis=0)
            q, k, vt = _qkv(x, mod, g_pre, p["at_w_qkv"][j], gains, cos, sin, seq=seq, n_kv=n_kv)
            o = _flash(q, k, vt, batch=batch, seq=seq, n_kv=n_kv)
            x = _down(o, p["at_w_o"], x, mod, g_post, layer=j, seq=seq, gate_idx=2)
        g = _up(x, mod, p["norm_ffn_pre"][i:i + 1], p["ffn_w_up"], None, p["ffn_conv_w"][i],
                p["ffn_conv_b"][i:i + 1], layer=i, seq=seq, sh_idx=3, sc_idx=4, glu=True, out_dtype=BF16)
        x = _down(g, p["ffn_w_down"], x, mod, p["norm_ffn_post"][i:i + 1], layer=i, seq=seq, gate_idx=5)
    return x.reshape(batch, seq, d)


def kernel(x_prompt, x_sample, c_prompt, c_sample, mod_w, mod_b, norm_mix_pre, norm_mix_post, norm_ffn_pre,
           norm_ffn_post, ffn_w_up, ffn_conv_w, ffn_conv_b, ffn_w_down, pool_w, pool_scale, hy_w_in, hy_b_in,
           hy_conv_w, hy_conv_b, hy_f_w1, hy_f_b1, hy_f_w2, hy_f_b2, hy_f_w3, hy_f_freq, hy_skip, hy_w_out,
           at_w_qkv, at_q_gain, at_k_gain, at_w_o):
    nb = c_prompt.shape[0]
    ns = c_sample.shape[0]
    assert nb + ns <= MOD_ROWS
    c8 = jnp.concatenate([c_prompt, c_sample, jnp.zeros((MOD_ROWS - nb - ns, c_prompt.shape[1]), F32)], axis=0)
    modall = _modulation(c8, mod_w, mod_b)
    row = lambda v: v[:, None, :]
    p = dict(
        norm_mix_pre=norm_mix_pre, norm_mix_post=norm_mix_post, norm_ffn_pre=norm_ffn_pre,
        norm_ffn_post=norm_ffn_post,
        ffn_w_up=ffn_w_up.astype(BF16), ffn_conv_w=ffn_conv_w, ffn_conv_b=ffn_conv_b,
        ffn_w_down=ffn_w_down.astype(BF16),
        pool_w=pool_w.astype(BF16), pool_scale=pool_scale,
        hy=dict(
            w_in=hy_w_in.astype(BF16), b_in=row(hy_b_in), conv_w=hy_conv_w, conv_b=row(hy_conv_b),
            f_w1=jnp.pad(hy_f_w1, ((0, 0), (0, HYENA_EMB_PAD - HYENA_EMB), (0, 0))), f_b1=row(hy_f_b1),
            f_w2=hy_f_w2, f_b2=row(hy_f_b2), f_w3=hy_f_w3, f_freq=row(hy_f_freq), skip=hy_skip,
            w_out=hy_w_out.astype(BF16)),
        at_w_qkv=at_w_qkv.astype(BF16), at_q_gain=at_q_gain, at_k_gain=at_k_gain, at_w_o=at_w_o.astype(BF16),
    )
    y_prompt = _trunk(x_prompt, modall, 0, p)
    y_sample = _trunk(x_sample, modall, nb, p)
    return (y_prompt, y_sample)
```
